```python
import math
import jax
import jax.numpy as jnp
from jax import lax
import numpy as np

D_MODEL = 1024
BATCH = 4
SEQ = 8192
DEPTH = 2

CTX_LEN = 256
GRID_W = 64
EPS = 1e-6
ROPE_BASE = 10000.0
ROT_DIM = 64
Q_BLOCK = 128

DA_HEADS = 4
DA_HEAD_DIM = ROT_DIM
DA_V_DIM = 2 * DA_HEAD_DIM
DA_QK_COLS = DA_HEADS * 2 * DA_HEAD_DIM
DA_WIDTH = DA_HEADS * DA_V_DIM

SSD_HEADS = 8
SSD_HEAD_DIM = 64
SSD_INNER = SSD_HEADS * SSD_HEAD_DIM
SSD_GROUPS = 2
SSD_STATE = 128
SSD_CONV = 5
SSD_CHUNK = 128
SSD_XBC = SSD_INNER + 2 * SSD_GROUPS * SSD_STATE

MLA_HEADS = 4
MLA_Q_LORA = 384
MLA_KV_LORA = 256
MLA_NOPE = 128
MLA_ROPE = ROT_DIM
MLA_V = 128
MLA_WIDTH = MLA_HEADS * MLA_V

HY_WIDTH = 512
HY_ORDER = 2
HY_SHORT = 3
HY_EMB = 33
HY_HIDDEN = 64
HY_TARGET = 1e-2
HY_DECAY_SHORT = 0.3
HY_DECAY_LONG = 1.5

N_BRANCH = 4
COLS_DA = 2 * DA_QK_COLS + DA_WIDTH
COLS_SSD = SSD_INNER + SSD_XBC + 2 * SSD_HEADS
COLS_MLA = MLA_Q_LORA + MLA_KV_LORA + MLA_ROPE
COLS_HY = (HY_ORDER + 1) * HY_WIDTH
COLS_GATE = N_BRANCH * D_MODEL
IN_COLS = COLS_DA + COLS_SSD + COLS_MLA + COLS_HY + COLS_GATE

FFN_HIDDEN = -(-8 * D_MODEL // (3 * 256)) * 256

kernel_name = 'hybrid_diffusion_prefix_block'


def rmsnorm(x, g):
    xf = x.astype(jnp.float32)
    y = xf * lax.rsqrt(jnp.mean(xf * xf, axis=-1, keepdims=True) + EPS)
    return (y * g.astype(jnp.float32)).astype(x.dtype)


def modulate(h, shift, scale):
    return h * (1.0 + scale) + shift


def dwconv(u, w, b):
    k, ch = w.shape
    y = lax.conv_general_dilated(u, w[:, None, :].astype(u.dtype), window_strides=(1,),
                                 padding=[(k // 2, k // 2)],
                                 dimension_numbers=('NWC', 'WIO', 'NWC'),
                                 feature_group_count=ch)
    return y + b.astype(u.dtype)


def split_cols(p):
    offs = np.cumsum([COLS_DA, COLS_SSD, COLS_MLA, COLS_HY]).tolist()
    return jnp.split(p, offs, axis=-1)


def axial_rope_tables(rows):
    t = jnp.arange(rows * GRID_W)
    pos_row = (t // GRID_W).astype(jnp.float32)
    pos_col = (t % GRID_W).astype(jnp.float32)
    quarter = ROT_DIM // 4
    inv = ROPE_BASE ** (-jnp.arange(quarter, dtype=jnp.float32) / quarter)
    ang = jnp.concatenate([pos_row[:, None] * inv, pos_col[:, None] * inv], axis=-1)
    return jnp.cos(ang), jnp.sin(ang)


def apply_rope(x, cos, sin):
    shape = (1, x.shape[1]) + (1,) * (x.ndim - 3) + (x.shape[-1] // 2,)
    c = cos.reshape(shape).astype(x.dtype)
    s = sin.reshape(shape).astype(x.dtype)
    x1, x2 = jnp.split(x, 2, axis=-1)
    return jnp.concatenate([x1 * c - x2 * s, x2 * c + x1 * s], axis=-1)


def sweep_query_blocks(fn, *qs):
    b, n = qs[0].shape[:2]
    nb = n // Q_BLOCK
    blocks = tuple(jnp.moveaxis(q.reshape((b, nb, Q_BLOCK) + q.shape[2:]), 1, 0) for q in qs)
    out = lax.map(lambda args: fn(*args), blocks)
    return jnp.moveaxis(out, 0, 1).reshape((b, n) + out.shape[3:])


def diff_attention(q, k, v, lam):
    scale = DA_HEAD_DIM ** -0.5

    def block(qb):
        s = jnp.einsum('bqhcd,bkhcd->bhcqk', qb, k)
        p = jax.nn.softmax(s.astype(jnp.float32) * scale, axis=-1)
        w = (p[:, :, 0] - lam * p[:, :, 1]).astype(v.dtype)
        return jnp.einsum('bhqk,bkhe->bqhe', w, v)

    return sweep_query_blocks(block, q)


def da_project(cols, rope_cs):
    b, n = cols.shape[:2]
    q, k, v = jnp.split(cols, [DA_QK_COLS, 2 * DA_QK_COLS], axis=-1)
    q = q.reshape(b, n, DA_HEADS, 2, DA_HEAD_DIM)
    k = k.reshape(b, n, DA_HEADS, 2, DA_HEAD_DIM)
    v = v.reshape(b, n, DA_HEADS, DA_V_DIM)
    if rope_cs is not None:
        q = apply_rope(q, *rope_cs)
        k = apply_rope(k, *rope_cs)
    return q, k, v


def diff_attn_branch(cols_lat, cols_ctx, lp, lam_init, rope_cs, need_ctx):
    q_l, k_l, v_l = da_project(cols_lat, rope_cs)
    q_c, k_c, v_c = da_project(cols_ctx, None)
    lv = lp['da_lambda'].astype(jnp.float32)
    lam = jnp.exp(jnp.sum(lv[0] * lv[1])) - jnp.exp(jnp.sum(lv[2] * lv[3])) + lam_init

    def finish(o):
        o = rmsnorm(o, lp['da_subln']) * (1.0 - lam_init)
        return o.reshape(o.shape[0], o.shape[1], DA_WIDTH)

    o_lat = finish(diff_attention(q_l, jnp.concatenate([k_c, k_l], axis=1),
                                  jnp.concatenate([v_c, v_l], axis=1), lam))
    o_ctx = finish(diff_attention(q_c, k_c, v_c, lam)) if need_ctx else None
    return o_lat, o_ctx


def segsum(a):
    t = a.shape[-1]
    cs = jnp.cumsum(a, axis=-1)
    d = cs[..., :, None] - cs[..., None, :]
    return jnp.where(jnp.tril(jnp.ones((t, t), dtype=bool)), d, -jnp.inf)


def ssd_scan(xh, dt, a_head, bm, cm, init):
    b, l, h, p = xh.shape
    rep = h // bm.shape[2]
    nc = l // SSD_CHUNK
    bh = jnp.repeat(bm, rep, axis=2).reshape(b, nc, SSD_CHUNK, h, -1)
    ch = jnp.repeat(cm, rep, axis=2).reshape(b, nc, SSD_CHUNK, h, -1)
    xdt = (xh * dt[..., None]).reshape(b, nc, SSD_CHUNK, h, p)
    a = jnp.moveaxis((dt * a_head).reshape(b, nc, SSD_CHUNK, h), 3, 1)
    a_cum = jnp.cumsum(a, axis=-1)
    scores = jnp.einsum('bclhn,bcshn->bhcls', ch, bh) * jnp.exp(segsum(a))
    y_diag = jnp.einsum('bhcls,bcshp->bclhp', scores, xdt)
    decay_to_end = jnp.exp(a_cum[..., -1:] - a_cum)
    chunk_states = jnp.einsum('bclhn,bhcl,bclhp->bchpn', bh, decay_to_end, xdt)
    chunk_decay = jnp.exp(a_cum[..., -1])

    def step(s, inp):
        st, dec = inp
        return s * dec[..., None, None] + st, s

    final, prev = lax.scan(step, init, (jnp.moveaxis(chunk_states, 1, 0), jnp.moveaxis(chunk_decay, 2, 0)))
    y_off = jnp.einsum('bclhn,cbhpn,bhcl->bclhp', ch, prev, jnp.exp(a_cum))
    return (y_diag + y_off).reshape(b, l, h, p), final


def ssd_sequence(xbc, dt_raw, lp, init_f, init_b):
    f32 = jnp.float32
    b, n = xbc.shape[:2]
    xbc = jax.nn.silu(dwconv(xbc, lp['ssd_conv_w'], lp['ssd_conv_b']).astype(f32))
    xs, bm, cm = jnp.split(xbc, [SSD_INNER, SSD_INNER + SSD_GROUPS * SSD_STATE], axis=-1)
    xh = xs.reshape(b, n, SSD_HEADS, SSD_HEAD_DIM)
    bm = bm.reshape(b, n, SSD_GROUPS, SSD_STATE)
    cm = cm.reshape(b, n, SSD_GROUPS, SSD_STATE)
    dt = jax.nn.softplus(dt_raw.astype(f32).reshape(b, n, 2, SSD_HEADS) + lp['ssd_dt_bias'].astype(f32))
    a = -jnp.exp(lp['ssd_a_log'].astype(f32))
    y_f, s_f = ssd_scan(xh, dt[:, :, 0], a[0], bm, cm, init_f)
    rev = lambda t: jnp.flip(t, axis=1)
    y_b, s_b = ssd_scan(rev(xh), rev(dt[:, :, 1]), a[1], rev(bm), rev(cm), init_b)
    y = y_f + rev(y_b) + lp['ssd_d'].astype(f32)[:, None] * xh
    return y.reshape(b, n, SSD_INNER), s_f, s_b


def ssd_branch(cols_lat, cols_ctx, lp, need_ctx):
    z_l, xbc_l, dt_l = jnp.split(cols_lat, [SSD_INNER, SSD_INNER + SSD_XBC], axis=-1)
    z_c, xbc_c, dt_c = jnp.split(cols_ctx, [SSD_INNER, SSD_INNER + SSD_XBC], axis=-1)
    zeros = jnp.zeros((cols_lat.shape[0], SSD_HEADS, SSD_HEAD_DIM, SSD_STATE), jnp.float32)
    y_c, s_f, s_b = ssd_sequence(xbc_c, dt_c, lp, zeros, zeros)
    y_l, _, _ = ssd_sequence(xbc_l, dt_l, lp, s_f, s_b)

    def finish(y, z):
        return rmsnorm(y * jax.nn.silu(z.astype(jnp.float32)), lp['ssd_norm']).astype(z.dtype)

    return finish(y_l, z_l), (finish(y_c, z_c) if need_ctx else None)


def mla_attention(q_nope, q_rope, k_nope, k_rope, v):
    scale = (MLA_NOPE + MLA_ROPE) ** -0.5

    def block(qn, qr):
        s = jnp.einsum('bqhd,bkhd->bhqk', qn, k_nope) + jnp.einsum('bqhr,bkr->bhqk', qr, k_rope)
        p = jax.nn.softmax(s.astype(jnp.float32) * scale, axis=-1).astype(v.dtype)
        return jnp.einsum('bhqk,bkhe->bqhe', p, v)

    return sweep_query_blocks(block, q_nope, q_rope)


def mla_project(cols, lp, rope_cs):
    b, n = cols.shape[:2]
    cq, ckv, kr = jnp.split(cols, [MLA_Q_LORA, MLA_Q_LORA + MLA_KV_LORA], axis=-1)
    q = (rmsnorm(cq, lp['mla_q_norm']) @ lp['mla_w_uq']).reshape(b, n, MLA_HEADS, MLA_NOPE + MLA_ROPE)
    kv = (rmsnorm(ckv, lp['mla_kv_norm']) @ lp['mla_w_ukv']).reshape(b, n, MLA_HEADS, MLA_NOPE + MLA_V)
    q_nope, q_rope = jnp.split(q, [MLA_NOPE], axis=-1)
    k_nope, v = jnp.split(kv, [MLA_NOPE], axis=-1)
    if rope_cs is not None:
        q_rope = apply_rope(q_rope, *rope_cs)
        kr = apply_rope(kr, *rope_cs)
    return q_nope, q_rope, k_nope, kr, v


def mla_branch(cols_lat, cols_ctx, lp, rope_cs, need_ctx):
    qn_l, qr_l, kn_l, kr_l, v_l = mla_project(cols_lat, lp, rope_cs)
    qn_c, qr_c, kn_c, kr_c, v_c = mla_project(cols_ctx, lp, None)
    b = cols_lat.shape[0]
    o_lat = mla_attention(qn_l, qr_l, jnp.concatenate([kn_c, kn_l], axis=1),
                          jnp.concatenate([kr_c, kr_l], axis=1), jnp.concatenate([v_c, v_l], axis=1))
    o_lat = o_lat.reshape(b, -1, MLA_WIDTH)
    o_ctx = mla_attention(qn_c, qr_c, kn_c, kr_c, v_c).reshape(b, -1, MLA_WIDTH) if need_ctx else None
    return o_lat, o_ctx


def hyena_filter_spectrum(n, lp):
    f32 = jnp.float32
    t = jnp.arange(n, dtype=f32)
    t_unit = t / (n - 1)
    bands = (HY_EMB - 1) // 2
    band_f = jnp.linspace(1e-4, bands - 1, bands, dtype=f32)
    w = 2.0 * math.pi * t / n
    feats = jnp.concatenate([t_unit[:, None], jnp.cos(w[:, None] * band_f), -jnp.sin(w[:, None] * band_f)], axis=-1)
    hid = jnp.sin(lp['hy_freq1'].astype(f32) * (feats @ lp['hy_w1'].astype(f32) + lp['hy_b1'].astype(f32)))
    hid = jnp.sin(lp['hy_freq2'].astype(f32) * (hid @ lp['hy_w2'].astype(f32) + lp['hy_b2'].astype(f32)))
    h = (hid @ lp['hy_w3'].astype(f32)).reshape(n, 2, HY_ORDER, HY_WIDTH)
    deltas = jnp.abs(jnp.linspace(math.log(HY_TARGET) / HY_DECAY_LONG, math.log(HY_TARGET) / HY_DECAY_SHORT,
                                  HY_WIDTH, dtype=f32))
    h = h * jnp.exp(-t_unit[:, None] * deltas)[:, None, None, :]
    fwd, bwd = h[:, 0], h[:, 1]
    g = jnp.concatenate([fwd[:1] + bwd[:1], fwd[1:], jnp.zeros_like(fwd[:1]), jnp.flip(bwd[1:], axis=0)], axis=0)
    g = g * lax.rsqrt(jnp.sum(g * g, axis=0, keepdims=True) + EPS)
    return jnp.fft.rfft(g, axis=0)


def fft_conv(u, spec):
    n = u.shape[1]
    y = jnp.fft.irfft(jnp.fft.rfft(u, n=2 * n, axis=1) * spec[None], n=2 * n, axis=1)
    return y[:, :n]


def hyena_sequence(cols, lp):
    n = cols.shape[1]
    u = dwconv(cols, lp['hy_conv_w'], lp['hy_conv_b']).astype(jnp.float32)
    v, x1, x2 = jnp.split(u, HY_ORDER + 1, axis=-1)
    spec = hyena_filter_spectrum(n, lp)
    bias = lp['hy_bias'].astype(jnp.float32)
    z = v
    for o, gate in enumerate((x1, x2)):
        z = gate * (fft_conv(z, spec[:, o]) + bias[o] * z)
    return z.astype(cols.dtype)


def hyena_branch(cols_lat, cols_ctx, lp, need_ctx):
    return hyena_sequence(cols_lat, lp), (hyena_sequence(cols_ctx, lp) if need_ctx else None)


def merge_branches(outs, gate_cols, w_brs, w_out):
    gates = jnp.split(jax.nn.sigmoid(gate_cols), N_BRANCH, axis=-1)
    mixed = sum(g * (o @ w) for g, o, w in zip(gates, outs, w_brs))
    return mixed @ w_out


def token_mixer(h_lat, h_ctx, lp, lam_init, rope_cs, need_ctx):
    da_l, ssd_l, mla_l, hy_l, gate_l = split_cols(h_lat @ lp['w_in'])
    da_c, ssd_c, mla_c, hy_c, gate_c = split_cols(h_ctx @ lp['w_in'])
    branches = (
        diff_attn_branch(da_l, da_c, lp, lam_init, rope_cs, need_ctx),
        ssd_branch(ssd_l, ssd_c, lp, need_ctx),
        mla_branch(mla_l, mla_c, lp, rope_cs, need_ctx),
        hyena_branch(hy_l, hy_c, lp, need_ctx),
    )
    w_brs = (lp['w_br_da'], lp['w_br_ssd'], lp['w_br_mla'], lp['w_br_hy'])
    y_lat = merge_branches([br[0] for br in branches], gate_l, w_brs, lp['w_out'])
    y_ctx = merge_branches([br[1] for br in branches], gate_c, w_brs, lp['w_out']) if need_ctx else None
    return y_lat, y_ctx


def swiglu(h, w1, w3, w2):
    return (jax.nn.silu(h @ w1) * (h @ w3)) @ w2


def setup_inputs(seed: int = 0) -> dict:
    key = jax.random.key(seed)
    keys = iter(jax.random.split(key, 64))
    f32 = jnp.float32
    L = DEPTH

    def normal(shape, scale):
        return scale * jax.random.normal(next(keys), shape, f32)

    def gain(shape):
        return 1.0 + normal(shape, 0.05)

    dt0 = jnp.exp(jax.random.uniform(next(keys), (L, 2, SSD_HEADS), f32, math.log(1e-3), math.log(1e-1)))
    a0 = jax.random.uniform(next(keys), (L, 2, SSD_HEADS), f32, 1.0, 16.0)
    return {
        'x': normal((BATCH, SEQ, D_MODEL), 1.0),
        'c': normal((BATCH, D_MODEL), 1.0),
        'ctx': normal((BATCH, CTX_LEN, D_MODEL), 1.0),
        'c_ctx': normal((D_MODEL,), 1.0),
        'mod_w': normal((L, D_MODEL, 6 * D_MODEL), 0.5 * D_MODEL ** -0.5),
        'mod_b': normal((L, 6 * D_MODEL), 0.02),
        'norm_mix_pre': gain((L, D_MODEL)),
        'norm_mix_post': gain((L, D_MODEL)),
        'norm_ffn_pre': gain((L, D_MODEL)),
        'norm_ffn_post': gain((L, D_MODEL)),
        'w_in': normal((L, D_MODEL, IN_COLS), D_MODEL ** -0.5),
        'da_lambda': normal((L, 4, DA_HEAD_DIM), 0.1),
        'da_subln': gain((L, DA_V_DIM)),
        'ssd_conv_w': normal((L, SSD_CONV, SSD_XBC), SSD_CONV ** -0.5),
        'ssd_conv_b': normal((L, SSD_XBC), 0.02),
        'ssd_a_log': jnp.log(a0),
        'ssd_dt_bias': dt0 + jnp.log(-jnp.expm1(-dt0)),
        'ssd_d': 1.0 + normal((L, SSD_HEADS), 0.1),
        'ssd_norm': gain((L, SSD_INNER)),
        'mla_q_norm': gain((L, MLA_Q_LORA)),
        'mla_w_uq': normal((L, MLA_Q_LORA, MLA_HEADS * (MLA_NOPE + MLA_ROPE)), MLA_Q_LORA ** -0.5),
        'mla_kv_norm': gain((L, MLA_KV_LORA)),
        'mla_w_ukv': normal((L, MLA_KV_LORA, MLA_HEADS * (MLA_NOPE + MLA_V)), MLA_KV_LORA ** -0.5),
        'hy_conv_w': normal((L, HY_SHORT, COLS_HY), HY_SHORT ** -0.5),
        'hy_conv_b': normal((L, COLS_HY), 0.02),
        'hy_w1': normal((L, HY_EMB, HY_HIDDEN), HY_EMB ** -0.5),
        'hy_b1': normal((L, HY_HIDDEN), 0.02),
        'hy_freq1': 1.0 + normal((L, HY_HIDDEN), 0.1),
        'hy_w2': normal((L, HY_HIDDEN, HY_HIDDEN), HY_HIDDEN ** -0.5),
        'hy_b2': normal((L, HY_HIDDEN), 0.02),
        'hy_freq2': 1.0 + normal((L, HY_HIDDEN), 0.1),
        'hy_w3': normal((L, HY_HIDDEN, 2 * HY_ORDER * HY_WIDTH), HY_HIDDEN ** -0.5),
        'hy_bias': normal((L, HY_ORDER, HY_WIDTH), 0.5),
        'w_br_da': normal((L, DA_WIDTH, D_MODEL), DA_WIDTH ** -0.5),
        'w_br_ssd': normal((L, SSD_INNER, D_MODEL), SSD_INNER ** -0.5),
        'w_br_mla': normal((L, MLA_WIDTH, D_MODEL), MLA_WIDTH ** -0.5),
        'w_br_hy': normal((L, HY_WIDTH, D_MODEL), HY_WIDTH ** -0.5),
        'w_out': normal((L, D_MODEL, D_MODEL), D_MODEL ** -0.5),
        'ffn_w1': normal((L, D_MODEL, FFN_HIDDEN), D_MODEL ** -0.5),
        'ffn_w3': normal((L, D_MODEL, FFN_HIDDEN), D_MODEL ** -0.5),
        'ffn_w2': normal((L, FFN_HIDDEN, D_MODEL), FFN_HIDDEN ** -0.5),
    }


def reference(x, c, ctx, c_ctx, mod_w, mod_b, norm_mix_pre, norm_mix_post, norm_ffn_pre,
              norm_ffn_post, w_in, da_lambda, da_subln, ssd_conv_w, ssd_conv_b, ssd_a_log,
              ssd_dt_bias, ssd_d, ssd_norm, mla_q_norm, mla_w_uq, mla_kv_norm, mla_w_ukv,
              hy_conv_w, hy_conv_b, hy_w1, hy_b1, hy_freq1, hy_w2, hy_b2, hy_freq2, hy_w3,
              hy_bias, w_br_da, w_br_ssd, w_br_mla, w_br_hy, w_out, ffn_w1, ffn_w3, ffn_w2):
    n_lat = x.shape[1]
    rows = n_lat // GRID_W
    rope_cs = axial_rope_tables(rows)
    c_act = jax.nn.silu(c)
    cc_act = jax.nn.silu(c_ctx)
    x_lat, x_ctx = x, ctx
    for l in range(DEPTH):
        last = l == DEPTH - 1
        lp = {
            'w_in': w_in[l], 'da_lambda': da_lambda[l], 'da_subln': da_subln[l],
            'ssd_conv_w': ssd_conv_w[l], 'ssd_conv_b': ssd_conv_b[l], 'ssd_a_log': ssd_a_log[l],
            'ssd_dt_bias': ssd_dt_bias[l], 'ssd_d': ssd_d[l], 'ssd_norm': ssd_norm[l],
            'mla_q_norm': mla_q_norm[l], 'mla_w_uq': mla_w_uq[l], 'mla_kv_norm': mla_kv_norm[l],
            'mla_w_ukv': mla_w_ukv[l], 'hy_conv_w': hy_conv_w[l], 'hy_conv_b': hy_conv_b[l],
            'hy_w1': hy_w1[l], 'hy_b1': hy_b1[l], 'hy_freq1': hy_freq1[l], 'hy_w2': hy_w2[l],
            'hy_b2': hy_b2[l], 'hy_freq2': hy_freq2[l], 'hy_w3': hy_w3[l], 'hy_bias': hy_bias[l],
            'w_br_da': w_br_da[l], 'w_br_ssd': w_br_ssd[l], 'w_br_mla': w_br_mla[l],
            'w_br_hy': w_br_hy[l], 'w_out': w_out[l],
        }
        lam_init = 0.8 - 0.6 * math.exp(-0.3 * l)
        mod_lat = jnp.split((c_act @ mod_w[l] + mod_b[l])[:, None, :], 6, axis=-1)
        mod_ctx = jnp.split(cc_act @ mod_w[l] + mod_b[l], 6, axis=-1)
        h_lat = modulate(rmsnorm(x_lat, norm_mix_pre[l]), mod_lat[0], mod_lat[1])
        h_ctx = modulate(rmsnorm(x_ctx, norm_mix_pre[l]), mod_ctx[0], mod_ctx[1])
        y_lat, y_ctx = token_mixer(h_lat, h_ctx, lp, lam_init, rope_cs, not last)
        x_lat = x_lat + mod_lat[2] * rmsnorm(y_lat, norm_mix_post[l])
        f_lat = swiglu(modulate(rmsnorm(x_lat, norm_ffn_pre[l]), mod_lat[3], mod_lat[4]),
                       ffn_w1[l], ffn_w3[l], ffn_w2[l])
        x_lat = x_lat + mod_lat[5] * rmsnorm(f_lat, norm_ffn_post[l])
        if not last:
            x_ctx = x_ctx + mod_ctx[2] * rmsnorm(y_ctx, norm_mix_post[l])
            f_ctx = swiglu(modulate(rmsnorm(x_ctx, norm_ffn_pre[l]), mod_ctx[3], mod_ctx[4]),
                           ffn_w1[l], ffn_w3[l], ffn_w2[l])
            x_ctx = x_ctx + mod_ctx[5] * rmsnorm(f_ctx, norm_ffn_post[l])
    return x_lat
```

```python
import functools
import math

import jax
import jax.numpy as jnp
import numpy as np
from jax import lax
from jax.experimental import pallas as pl
from jax.experimental.pallas import tpu as pltpu

F32 = jnp.float32
BF16 = jnp.bfloat16
HIGHEST = lax.Precision.HIGHEST

D_MODEL = 1024
DEPTH = 2
GRID_W = 64
EPS = 1e-6
ROPE_BASE = 10000.0
ROT_DIM = 64

DA_HEADS = 4
DA_HEAD_DIM = 64
DA_V_DIM = 128
DA_QK_COLS = 512
DA_WIDTH = 512
COLS_DA = 1536

SSD_HEADS = 8
SSD_HEAD_DIM = 64
SSD_INNER = 512
SSD_GROUPS = 2
SSD_STATE = 128
SSD_CONV = 5
SSD_CHUNK = 128
SSD_XBC = 1024
COLS_SSD = 1552

MLA_HEADS = 4
MLA_Q_LORA = 384
MLA_KV_LORA = 256
MLA_NOPE = 128
MLA_ROPE = 64
MLA_V = 128
MLA_WIDTH = 512
COLS_MLA = 704

HY_WIDTH = 512
HY_ORDER = 2
HY_SHORT = 3
HY_EMB = 33
HY_TARGET = 1e-2
HY_DECAY_SHORT = 0.3
HY_DECAY_LONG = 1.5
COLS_HY = 1536

N_BRANCH = 4
COLS_GATE = 4096
FFN_HIDDEN = 2816

LANES = 128
SUBLANES = 8
VMEM_LIMIT_BYTES = 48 * 2 ** 20
NEG_BIG = -1e30

ROW_TILE = 512
ATTN_TQ = 512
ATTN_TK = 512
FFN_TH = 1408
DFT_B = 128


def _params(sem):
    return pltpu.CompilerParams(dimension_semantics=sem, vmem_limit_bytes=VMEM_LIMIT_BYTES)


def _full(shape):
    nd = len(shape)
    return pl.BlockSpec(shape, lambda *_: (0,) * nd)


def _dot(a, b):
    return jnp.dot(a, b, preferred_element_type=F32)


def _dot_nt(a, b):
    return lax.dot_general(a, b, (((1,), (1,)), ((), ())), preferred_element_type=F32)


def _dot_tn(a, b):
    return lax.dot_general(a, b, (((0,), (0,)), ((), ())), preferred_element_type=F32)


def _dot_hi(a, b):
    return jnp.dot(a, b, precision=HIGHEST, preferred_element_type=F32)


def _rms(x, g):
    return x * lax.rsqrt(jnp.mean(x * x, axis=-1, keepdims=True) + EPS) * g


def _norm_mod(x, g, shift, scale):
    return _rms(x, g) * (1.0 + scale) + shift


def _silu(x):
    return x * jax.nn.sigmoid(x)


def _rope128(x, cosf, sins, first_half):
    partner = jnp.where(first_half, pltpu.roll(x, 96, 1), pltpu.roll(x, 32, 1))
    return x * cosf + partner * sins


def _first_half_mask(rows):
    return (lax.broadcasted_iota(jnp.int32, (rows, LANES), 1) & 32) == 0


def _mod_kernel(c_ref, w_ref, b_ref, o_ref):
    act = _silu(c_ref[...]).astype(BF16)
    o_ref[...] = _dot(act, w_ref[...]) + b_ref[...]


def _modulation(c_rows, w, b):
    n = w.shape[1] // D_MODEL
    return pl.pallas_call(
        _mod_kernel,
        grid=(n,),
        in_specs=[_full((SUBLANES, D_MODEL)),
                  pl.BlockSpec((D_MODEL, D_MODEL), lambda j: (0, j)),
                  pl.BlockSpec((1, D_MODEL), lambda j: (0, j))],
        out_specs=pl.BlockSpec((SUBLANES, D_MODEL), lambda j: (0, j)),
        out_shape=jax.ShapeDtypeStruct((SUBLANES, w.shape[1]), F32),
        compiler_params=_params(("parallel",)),
        name="modulation",
    )(c_rows, w, b)


def _inproj_call(kernel, x, g, shift, scale, extra, extra_specs, out_dims, out_dtypes, name):
    b, l, _ = x.shape
    tm = min(ROW_TILE, l)
    row = lambda d: pl.BlockSpec((1, tm, d), lambda bi, i: (bi, i, 0))
    vec = pl.BlockSpec((1, 1, D_MODEL), lambda bi, i: (bi, 0, 0))
    return pl.pallas_call(
        kernel,
        grid=(b, l // tm),
        in_specs=[row(D_MODEL), _full((1, D_MODEL)), vec, vec] + extra_specs(tm),
        out_specs=[row(d) for d in out_dims],
        out_shape=[jax.ShapeDtypeStruct((b, l, d), dt) for d, dt in zip(out_dims, out_dtypes)],
        compiler_params=_params(("parallel", "parallel")),
        name=name,
    )(x, g, shift, scale, *extra)


def _rope_specs(tm):
    return [pl.BlockSpec((tm, LANES), lambda bi, i: (i, 0))] * 2


def _inproj_da_kernel(*refs, rope):
    if rope:
        x_ref, g_ref, sh_ref, sc_ref, w_ref, cos_ref, sin_ref, q_ref, k_ref, v_ref = refs
    else:
        x_ref, g_ref, sh_ref, sc_ref, w_ref, q_ref, k_ref, v_ref = refs
    h = _norm_mod(x_ref[0], g_ref[...], sh_ref[0], sc_ref[0]).astype(BF16)
    tm = h.shape[0]
    first = _first_half_mask(tm)
    scale = DA_HEAD_DIM ** -0.5
    for out_ref, c0, is_q, roped in ((q_ref, 0, True, True), (k_ref, 512, False, True), (v_ref, 1024, False, False)):
        res = _dot(h, w_ref[:, c0:c0 + 512])
        for i in range(4):
            t = res[:, i * LANES:(i + 1) * LANES]
            if rope and roped:
                t = _rope128(t, cos_ref[...], sin_ref[...], first)
            if is_q:
                t = t * scale
            out_ref[0, :, i * LANES:(i + 1) * LANES] = t.astype(BF16)


def _inproj_da(x, g, shift, scale, w, rope_tabs):
    rope = rope_tabs is not None
    extra = [w] + (list(rope_tabs) if rope else [])
    specs = lambda tm: [_full(w.shape)] + (_rope_specs(tm) if rope else [])
    return _inproj_call(functools.partial(_inproj_da_kernel, rope=rope), x, g, shift, scale, extra, specs,
                        (512, 512, 512), (BF16, BF16, BF16), "inproj_da")


def _inproj_mla_kernel(*refs, rope):
    if rope:
        (x_ref, g_ref, sh_ref, sc_ref, w_ref, gq_ref, wuq_ref, gkv_ref, wukv_ref, cos_ref, sin_ref,
         q_ref, k_ref, v_ref) = refs
    else:
        (x_ref, g_ref, sh_ref, sc_ref, w_ref, gq_ref, wuq_ref, gkv_ref, wukv_ref,
         q_ref, k_ref, v_ref) = refs
    h = _norm_mod(x_ref[0], g_ref[...], sh_ref[0], sc_ref[0]).astype(BF16)
    tm = h.shape[0]
    first = _first_half_mask(tm)
    res = _dot(h, w_ref[...])
    cq = _rms(res[:, :MLA_Q_LORA], gq_ref[...]).astype(BF16)
    ckv = _rms(res[:, MLA_Q_LORA:MLA_Q_LORA + MLA_KV_LORA], gkv_ref[...]).astype(BF16)
    kr = res[:, MLA_Q_LORA + MLA_KV_LORA:]
    q = _dot(cq, wuq_ref[...])
    kv = _dot(ckv, wukv_ref[...])
    if rope:
        kr = _rope128(kr, cos_ref[...], sin_ref[...], first)
    kr = kr.astype(BF16)
    scale = (MLA_NOPE + MLA_ROPE) ** -0.5
    for hh in range(MLA_HEADS):
        c0 = hh * 256
        qr = q[:, c0 + LANES:c0 + 256]
        if rope:
            qr = _rope128(qr, cos_ref[...], sin_ref[...], first)
        q_ref[0, :, c0:c0 + LANES] = (q[:, c0:c0 + LANES] * scale).astype(BF16)
        q_ref[0, :, c0 + LANES:c0 + 256] = (qr * scale).astype(BF16)
        k_ref[0, :, c0:c0 + LANES] = kv[:, c0:c0 + LANES].astype(BF16)
        k_ref[0, :, c0 + LANES:c0 + 256] = kr
        v_ref[0, :, hh * LANES:(hh + 1) * LANES] = kv[:, c0 + LANES:c0 + 256].astype(BF16)


def _inproj_mla(x, g, shift, scale, w, gq, wuq, gkv, wukv, rope_tabs):
    rope = rope_tabs is not None
    extra = [w, gq, wuq, gkv, wukv] + (list(rope_tabs) if rope else [])
    specs = lambda tm: [_full(a.shape) for a in (w, gq, wuq, gkv, wukv)] + (_rope_specs(tm) if rope else [])
    return _inproj_call(functools.partial(_inproj_mla_kernel, rope=rope), x, g, shift, scale, extra, specs,
                        (1024, 1024, 512), (BF16, BF16, BF16), "inproj_mla")


def _inproj_ssd_kernel(x_ref, g_ref, sh_ref, sc_ref, w_ref, z_ref, xbc_ref, dt_ref):
    h = _norm_mod(x_ref[0], g_ref[...], sh_ref[0], sc_ref[0]).astype(BF16)
    z_ref[0] = _dot(h, w_ref[:, 0:512])
    xbc_ref[0] = _dot(h, w_ref[:, 512:1536])
    dt_ref[0] = _dot(h, w_ref[:, 1536:1664])


def _inproj_ssd(x, g, shift, scale, w):
    return _inproj_call(_inproj_ssd_kernel, x, g, shift, scale, [w], lambda tm: [_full(w.shape)],
                        (512, 1024, LANES), (F32, F32, F32), "inproj_ssd")


def _inproj_plain_kernel(x_ref, g_ref, sh_ref, sc_ref, w_ref, o_ref, *, gate):
    h = _norm_mod(x_ref[0], g_ref[...], sh_ref[0], sc_ref[0]).astype(BF16)
    n = w_ref.shape[1]
    step = 512
    for c0 in range(0, n, step):
        res = _dot(h, w_ref[:, c0:c0 + step])
        if gate:
            res = jax.nn.sigmoid(res)
        o_ref[0, :, c0:c0 + step] = res.astype(o_ref.dtype)


def _inproj_plain(x, g, shift, scale, w, gate):
    return _inproj_call(functools.partial(_inproj_plain_kernel, gate=gate), x, g, shift, scale, [w],
                        lambda tm: [_full(w.shape)], (w.shape[1],), (BF16 if gate else F32,),
                        "inproj_gate" if gate else "inproj_hy")[0]


def _flash_kernel(*refs, ncomp, has_lat, da_finish, lam_init):
    q_ref, kc_ref, vc_ref = refs[:3]
    pos = 3
    if has_lat:
        kl_ref, vl_ref = refs[3:5]
        pos = 5
    if da_finish:
        lam_ref, sub_ref = refs[pos:pos + 2]
        pos += 2
    o_ref, m_ref, l_ref, acc_ref = refs[pos:pos + 4]
    j = pl.program_id(3)
    nk = pl.num_programs(3)
    q = q_ref[0]
    tq = q.shape[0]
    if ncomp == 2:
        lo = lax.broadcasted_iota(jnp.int32, (tq, LANES), 1) < DA_HEAD_DIM
        zero = jnp.zeros_like(q)
        qs = (jnp.where(lo, q, zero), jnp.where(lo, zero, q))
    else:
        qs = (q,)

    def attend(k, v):
        for c in range(ncomp):
            s = _dot_nt(qs[c], k)
            m_prev = m_ref[c]
            m_new = jnp.maximum(m_prev, jnp.max(s, axis=-1, keepdims=True))
            alpha = jnp.exp(m_prev - m_new)
            p = jnp.exp(s - m_new)
            l_ref[c] = alpha * l_ref[c] + jnp.sum(p, axis=-1, keepdims=True)
            acc_ref[c] = alpha * acc_ref[c] + _dot(p.astype(BF16), v)
            m_ref[c] = m_new

    @pl.when(j == 0)
    def _():
        m_ref[...] = jnp.full(m_ref.shape, NEG_BIG, F32)
        l_ref[...] = jnp.zeros(l_ref.shape, F32)
        acc_ref[...] = jnp.zeros(acc_ref.shape, F32)
        attend(kc_ref[0], vc_ref[0])

    if has_lat:
        attend(kl_ref[0], vl_ref[0])

    @pl.when(j == nk - 1)
    def _():
        o = acc_ref[0] / l_ref[0]
        if da_finish:
            o = o - lam_ref[...] * (acc_ref[1] / l_ref[1])
            o = _rms(o, sub_ref[...]) * (1.0 - lam_init)
        o_ref[0] = o.astype(o_ref.dtype)


def _flash(q, k_ctx, v_ctx, k_lat, v_lat, *, ncomp, heads, dq, da=None):
    b, lq, _ = q.shape
    lc = k_ctx.shape[1]
    has_lat = k_lat is not None
    tq = min(ATTN_TQ, lq)
    nq = lq // tq
    if has_lat:
        tk = min(ATTN_TK, k_lat.shape[1])
        nk = k_lat.shape[1] // tk
    else:
        nk = 1
    in_specs = [pl.BlockSpec((1, tq, dq), lambda bi, h, i, j: (bi, i, h)),
                pl.BlockSpec((1, lc, dq), lambda bi, h, i, j: (bi, 0, h)),
                pl.BlockSpec((1, lc, LANES), lambda bi, h, i, j: (bi, 0, h))]
    args = [q, k_ctx, v_ctx]
    if has_lat:
        in_specs += [pl.BlockSpec((1, tk, dq), lambda bi, h, i, j: (bi, j, h)),
                     pl.BlockSpec((1, tk, LANES), lambda bi, h, i, j: (bi, j, h))]
        args += [k_lat, v_lat]
    lam_init = 0.0
    if da is not None:
        lam, subln, lam_init = da
        in_specs += [_full((1, 1)), _full((1, LANES))]
        args += [lam, subln]
    kernel = functools.partial(_flash_kernel, ncomp=ncomp, has_lat=has_lat, da_finish=da is not None,
                               lam_init=lam_init)
    return pl.pallas_call(
        kernel,
        grid=(b, heads, nq, nk),
        in_specs=in_specs,
        out_specs=pl.BlockSpec((1, tq, LANES), lambda bi, h, i, j: (bi, i, h)),
        out_shape=jax.ShapeDtypeStruct((b, lq, heads * LANES), BF16),
        scratch_shapes=[pltpu.VMEM((ncomp, tq, 1), F32), pltpu.VMEM((ncomp, tq, 1), F32),
                        pltpu.VMEM((ncomp, tq, LANES), F32)],
        compiler_params=_params(("parallel", "parallel", "parallel", "arbitrary")),
        name="flash_da" if ncomp == 2 else "flash_mla",
    )(*args)


def _dwconv_kernel(x_ref, w_ref, b_ref, o_ref, pad_ref, *, taps, act, chunk):
    l = x_ref.shape[1]
    pad_ref[0:SUBLANES, :] = jnp.zeros((SUBLANES, LANES), F32)
    pad_ref[SUBLANES + l:2 * SUBLANES + l, :] = jnp.zeros((SUBLANES, LANES), F32)
    pad_ref[SUBLANES:SUBLANES + l, :] = x_ref[0]
    w = w_ref[...]
    bias = b_ref[...]

    def body(i, carry):
        base = pl.multiple_of(i * chunk, chunk)
        acc = jnp.zeros((chunk, LANES), F32) + bias
        for t in range(taps):
            acc = acc + w[t:t + 1, :] * pad_ref[pl.ds(base + SUBLANES + t - taps // 2, chunk), :]
        if act:
            acc = _silu(acc)
        o_ref[0, 0, pl.ds(base, chunk), :] = acc
        return carry

    lax.fori_loop(0, l // chunk, body, 0)


def _dwconv(x, w, bias, groups, act):
    b, l, c = x.shape
    taps = w.shape[0]
    cg = c // groups
    per = cg // LANES
    chunk = min(512, l)
    return pl.pallas_call(
        functools.partial(_dwconv_kernel, taps=taps, act=act, chunk=chunk),
        grid=(b, c // LANES),
        in_specs=[pl.BlockSpec((1, l, LANES), lambda bi, ct: (bi, 0, ct)),
                  pl.BlockSpec((taps, LANES), lambda bi, ct: (0, ct)),
                  pl.BlockSpec((1, LANES), lambda bi, ct: (0, ct))],
        out_specs=pl.BlockSpec((1, 1, l, LANES), lambda bi, ct: (ct // per, bi, 0, ct % per)),
        out_shape=jax.ShapeDtypeStruct((groups, b, l, cg), F32),
        scratch_shapes=[pltpu.VMEM((l + 2 * SUBLANES, LANES), F32)],
        compiler_params=_params(("parallel", "parallel")),
        name="dwconv",
    )(x, w, bias)


def _ssd_kernel(*refs, reverse, finish):
    if finish:
        (xc_ref, dt_ref, init_ref, dtb_ref, a_ref, yo_ref, z_ref, d_ref, g_ref,
         y_ref, fin_ref, st_ref) = refs
    else:
        xc_ref, dt_ref, init_ref, dtb_ref, a_ref, y_ref, fin_ref, st_ref = refs
    j = pl.program_id(1)
    nc = pl.num_programs(1)
    q = SSD_CHUNK

    @pl.when(j == 0)
    def _():
        st_ref[...] = init_ref[0]

    xc = xc_ref[0, 0]
    pre = dt_ref[0] + dtb_ref[...]
    dt = jnp.maximum(pre, 0.0) + jnp.log1p(jnp.exp(-jnp.abs(pre)))
    a = dt * a_ref[...]
    r = lax.broadcasted_iota(jnp.int32, (q, q), 0)
    c = lax.broadcasted_iota(jnp.int32, (q, q), 1)
    tri = (c >= r) if reverse else (c <= r)
    cum = _dot_hi(tri.astype(F32), a)
    cum_t = cum.T
    tot = cum[0:1, :] if reverse else cum[q - 1:q, :]
    off = SSD_HEADS if reverse else 0
    lane_lo = c < SSD_HEAD_DIM
    row_lo = r < SSD_HEAD_DIM
    ys = []
    for g in range(SSD_GROUPS):
        bm = xc[:, 512 + g * 128:512 + (g + 1) * 128].astype(BF16)
        cm = xc[:, 768 + g * 128:768 + (g + 1) * 128].astype(BF16)
        cb = _dot_nt(cm, bm)
        for pp in range(2):
            hp = 2 * g + pp
            l0 = off + 2 * hp
            l1 = l0 + 1
            col0, col1 = cum[:, l0:l0 + 1], cum[:, l1:l1 + 1]
            dec0 = jnp.exp(jnp.where(tri, col0 - cum_t[l0:l0 + 1, :], NEG_BIG))
            dec1 = jnp.exp(jnp.where(tri, col1 - cum_t[l1:l1 + 1, :], NEG_BIG))
            xdt = xc[:, hp * 128:(hp + 1) * 128] * jnp.where(lane_lo, dt[:, l0:l0 + 1], dt[:, l1:l1 + 1])
            xdt_b = xdt.astype(BF16)
            zero = jnp.zeros_like(xdt_b)
            y = (_dot((cb * dec0).astype(BF16), jnp.where(lane_lo, xdt_b, zero))
                 + _dot((cb * dec1).astype(BF16), jnp.where(lane_lo, zero, xdt_b)))
            colp = jnp.where(lane_lo, col0, col1)
            totp = jnp.where(lane_lo, tot[:, l0:l0 + 1], tot[:, l1:l1 + 1])
            contrib = _dot_tn((xdt * jnp.exp(totp - colp)).astype(BF16), bm)
            prev = st_ref[hp]
            y = y + _dot_nt(cm, prev.astype(BF16)) * jnp.exp(colp)
            st_ref[hp] = prev * jnp.exp(jnp.where(row_lo, tot[:, l0:l0 + 1], tot[:, l1:l1 + 1])) + contrib
            if finish:
                ys.append(y)
            else:
                y_ref[0, :, hp * 128:(hp + 1) * 128] = y

    if finish:
        yt = jnp.concatenate(ys, axis=1) + yo_ref[0] + d_ref[...] * xc[:, :SSD_INNER]
        y_ref[0] = _rms(yt * _silu(z_ref[0]), g_ref[...]).astype(y_ref.dtype)

    @pl.when(j == nc - 1)
    def _():
        fin_ref[0] = st_ref[...]


def _ssd_scan(xc, dt_raw, init, dtb, a_row, reverse, fin=None):
    _, b, l, _ = xc.shape
    nc = l // SSD_CHUNK
    cidx = (lambda j: nc - 1 - j) if reverse else (lambda j: j)
    rows = lambda d: pl.BlockSpec((1, SSD_CHUNK, d), lambda bi, j: (bi, cidx(j), 0))
    st_spec = pl.BlockSpec((1, 4, 128, 128), lambda bi, j: (bi, 0, 0, 0))
    in_specs = [pl.BlockSpec((1, 1, SSD_CHUNK, SSD_XBC), lambda bi, j: (0, bi, cidx(j), 0)),
                rows(LANES), st_spec, _full((1, LANES)), _full((1, LANES))]
    args = [xc, dt_raw, init, dtb, a_row]
    finish = fin is not None
    if finish:
        y_other, z, d_row, g = fin
        in_specs += [rows(SSD_INNER), rows(SSD_INNER), _full((1, SSD_INNER)), _full((1, SSD_INNER))]
        args += [y_other, z, d_row, g]
    return pl.pallas_call(
        functools.partial(_ssd_kernel, reverse=reverse, finish=finish),
        grid=(b, nc),
        in_specs=in_specs,
        out_specs=[rows(SSD_INNER), st_spec],
        out_shape=[jax.ShapeDtypeStruct((b, l, SSD_INNER), BF16 if finish else F32),
                   jax.ShapeDtypeStruct((b, 4, 128, 128), F32)],
        scratch_shapes=[pltpu.VMEM((4, 128, 128), F32)],
        compiler_params=_params(("parallel", "arbitrary")),
        name="ssd_bwd" if reverse else "ssd_fwd",
    )(*args)


def _hy_dft_a_kernel(z_ref, fa_ref, tw_ref, o_ref):
    z = z_ref[0, 0]
    res = _dot_hi(fa_ref[...], z)
    na2 = fa_ref.shape[0] // 2
    xr, xi = res[:na2], res[na2:]
    reps = z.shape[1] // LANES
    tr = jnp.concatenate([tw_ref[0, 0]] * reps, axis=1)
    ti = jnp.concatenate([tw_ref[0, 1]] * reps, axis=1)
    o_ref[0, 0] = xr * tr - xi * ti
    o_ref[0, 1] = xr * ti + xi * tr


def _hy_dft_b_kernel(y_ref, s_ref, mf_ref, mi_ref, o_ref):
    y = jnp.concatenate([y_ref[0, 0, 0], y_ref[0, 1, 0]], axis=0)
    x = _dot_hi(mf_ref[...], y)
    xr, xi = x[:DFT_B], x[DFT_B:]
    sr, si = s_ref[0, 0], s_ref[1, 0]
    z = jnp.concatenate([xr * sr - xi * si, xr * si + xi * sr], axis=0)
    out = _dot_hi(mi_ref[...], z)
    o_ref[0, 0, 0] = out[:DFT_B]
    o_ref[0, 1, 0] = out[DFT_B:]


def _hy_dft_c_kernel(y_ref, tw_ref, fc_ref, gate_ref, zp_ref, bias_ref, o_ref):
    yr, yi = y_ref[0, 0], y_ref[0, 1]
    reps = yr.shape[1] // LANES
    tr = jnp.concatenate([tw_ref[0, 0]] * reps, axis=1)
    ti = jnp.concatenate([tw_ref[0, 1]] * reps, axis=1)
    stacked = jnp.concatenate([yr * tr + yi * ti, yi * tr - yr * ti], axis=0)
    conv = _dot_hi(fc_ref[...], stacked)
    zp = zp_ref[0, 0]
    o_ref[0] = (gate_ref[0, 0] * (conv + bias_ref[...] * zp)).astype(o_ref.dtype)


def _hy_long_conv(zsrc, zgroup, u4, gate_group, spec_l, bias_row, tabs, out_dtype):
    fa, mf, mi, fc, tw = tabs
    gz, b, n, c = zsrc.shape
    na = n // DFT_B
    dft_a = 2 * na
    cols = DFT_B * c
    zv = zsrc.reshape(gz, b, na, cols)
    y1 = pl.pallas_call(
        _hy_dft_a_kernel,
        grid=(b, DFT_B),
        in_specs=[pl.BlockSpec((1, 1, na, c), lambda bi, jb: (zgroup, bi, 0, jb)),
                  _full(fa.shape),
                  pl.BlockSpec((1, 2, dft_a, LANES), lambda bi, jb: (jb, 0, 0, 0))],
        out_specs=pl.BlockSpec((1, 2, dft_a, c), lambda bi, jb: (bi, 0, 0, jb)),
        out_shape=jax.ShapeDtypeStruct((b, 2, dft_a, cols), F32),
        compiler_params=_params(("parallel", "parallel")),
        name="hyena_dft_a",
    )(zv, fa, tw)
    y2 = pl.pallas_call(
        _hy_dft_b_kernel,
        grid=(b, dft_a),
        in_specs=[pl.BlockSpec((1, 2, 1, DFT_B, c), lambda bi, ka: (bi, 0, ka, 0, 0)),
                  pl.BlockSpec((2, 1, DFT_B, c), lambda bi, ka: (0, ka, 0, 0)),
                  _full(mf.shape), _full(mi.shape)],
        out_specs=pl.BlockSpec((1, 2, 1, DFT_B, c), lambda bi, ka: (bi, 0, ka, 0, 0)),
        out_shape=jax.ShapeDtypeStruct((b, 2, dft_a, DFT_B, c), F32),
        compiler_params=_params(("parallel", "parallel")),
        name="hyena_dft_b",
    )(y1.reshape(b, 2, dft_a, DFT_B, c), spec_l, mf, mi)
    gv = u4.reshape(u4.shape[0], b, na, cols)
    out = pl.pallas_call(
        _hy_dft_c_kernel,
        grid=(b, DFT_B),
        in_specs=[pl.BlockSpec((1, 2, dft_a, c), lambda bi, jb: (bi, 0, 0, jb)),
                  pl.BlockSpec((1, 2, dft_a, LANES), lambda bi, jb: (jb, 0, 0, 0)),
                  _full(fc.shape),
                  pl.BlockSpec((1, 1, na, c), lambda bi, jb: (gate_group, bi, 0, jb)),
                  pl.BlockSpec((1, 1, na, c), lambda bi, jb: (zgroup, bi, 0, jb)),
                  _full((1, c))],
        out_specs=pl.BlockSpec((1, na, c), lambda bi, jb: (bi, 0, jb)),
        out_shape=jax.ShapeDtypeStruct((b, na, cols), out_dtype),
        compiler_params=_params(("parallel", "parallel")),
        name="hyena_dft_c",
    )(y2.reshape(b, 2, dft_a, cols), tw, fc, gv, zv, bias_row)
    return out.reshape(1, b, n, c)


def _hy_ctx_kernel(v_ref, x1_ref, x2_ref, s_ref, fw_ref, iv_ref, bias_ref, o_ref):
    z = v_ref[0, 0]
    n2 = fw_ref.shape[0] // 2
    for o, gate_ref in enumerate((x1_ref, x2_ref)):
        x = _dot_hi(fw_ref[...], z)
        xr, xi = x[:n2], x[n2:]
        sr, si = s_ref[o, 0], s_ref[o, 1]
        prod = jnp.concatenate([xr * sr - xi * si, xr * si + xi * sr], axis=0)
        conv = _dot_hi(iv_ref[...], prod)
        z = gate_ref[0, 0] * (conv + bias_ref[o:o + 1, :] * z)
    o_ref[0] = z.astype(o_ref.dtype)


def _hy_ctx(u4, spec_c, fw, iv, bias):
    _, b, n, c = u4.shape
    grp = lambda g: pl.BlockSpec((1, 1, n, c), lambda bi: (g, bi, 0, 0))
    return pl.pallas_call(
        _hy_ctx_kernel,
        grid=(b,),
        in_specs=[grp(0), grp(1), grp(2), _full(spec_c.shape), _full(fw.shape), _full(iv.shape), _full(bias.shape)],
        out_specs=pl.BlockSpec((1, n, c), lambda bi: (bi, 0, 0)),
        out_shape=jax.ShapeDtypeStruct((b, n, c), BF16),
        compiler_params=_params(("parallel",)),
        name="hyena_ctx",
    )(u4, u4, u4, spec_c, fw, iv, bias)


def _merge_kernel(x_ref, da_ref, ssd_ref, mla_ref, hy_ref, gate_ref, wbr_ref, wout_ref, gpost_ref, mg_ref, o_ref):
    mixed = None
    for k, br_ref in enumerate((da_ref, ssd_ref, mla_ref, hy_ref)):
        t = gate_ref[0, :, k * D_MODEL:(k + 1) * D_MODEL].astype(F32) * _dot(br_ref[0], wbr_ref[k])
        mixed = t if mixed is None else mixed + t
    y = _dot(mixed.astype(BF16), wout_ref[...])
    o_ref[0] = x_ref[0] + mg_ref[0] * _rms(y, gpost_ref[...])


def _merge(x, branches, gates, wbr, wout, gpost, mgate):
    b, l, _ = x.shape
    tm = min(ROW_TILE, l)
    row = lambda d: pl.BlockSpec((1, tm, d), lambda bi, i: (bi, i, 0))
    vec = pl.BlockSpec((1, 1, D_MODEL), lambda bi, i: (bi, 0, 0))
    return pl.pallas_call(
        _merge_kernel,
        grid=(b, l // tm),
        in_specs=[row(D_MODEL)] + [row(512)] * 4 + [row(COLS_GATE), _full(wbr.shape), _full(wout.shape),
                                                    _full((1, D_MODEL)), vec],
        out_specs=row(D_MODEL),
        out_shape=jax.ShapeDtypeStruct((b, l, D_MODEL), F32),
        compiler_params=_params(("parallel", "parallel")),
        name="merge",
    )(x, *branches, gates, wbr, wout, gpost, mgate)


def _ffn_kernel(x_ref, g_ref, sh_ref, sc_ref, w1_ref, w3_ref, w2_ref, gpost_ref, mg_ref, o_ref, h_ref, acc_ref):
    k = pl.program_id(2)

    @pl.when(k == 0)
    def _():
        h_ref[...] = _norm_mod(x_ref[0], g_ref[...], sh_ref[0], sc_ref[0]).astype(BF16)
        acc_ref[...] = jnp.zeros(acc_ref.shape, F32)

    h = h_ref[...]
    t = (_silu(_dot(h, w1_ref[...])) * _dot(h, w3_ref[...])).astype(BF16)
    acc_ref[...] += _dot(t, w2_ref[...])

    @pl.when(k == pl.num_programs(2) - 1)
    def _():
        o_ref[0] = x_ref[0] + mg_ref[0] * _rms(acc_ref[...], gpost_ref[...])


def _ffn(x, g, shift, scale, w1, w3, w2, gpost, mgate):
    b, l, _ = x.shape
    tm = min(ROW_TILE, l)
    nh = FFN_HIDDEN // FFN_TH
    row = pl.BlockSpec((1, tm, D_MODEL), lambda bi, i, k: (bi, i, 0))
    vec = pl.BlockSpec((1, 1, D_MODEL), lambda bi, i, k: (bi, 0, 0))
    return pl.pallas_call(
        _ffn_kernel,
        grid=(b, l // tm, nh),
        in_specs=[row, _full((1, D_MODEL)), vec, vec,
                  pl.BlockSpec((D_MODEL, FFN_TH), lambda bi, i, k: (0, k)),
                  pl.BlockSpec((D_MODEL, FFN_TH), lambda bi, i, k: (0, k)),
                  pl.BlockSpec((FFN_TH, D_MODEL), lambda bi, i, k: (k, 0)),
                  _full((1, D_MODEL)), vec],
        out_specs=row,
        out_shape=jax.ShapeDtypeStruct((b, l, D_MODEL), F32),
        scratch_shapes=[pltpu.VMEM((tm, D_MODEL), BF16), pltpu.VMEM((tm, D_MODEL), F32)],
        compiler_params=_params(("parallel", "parallel", "arbitrary")),
        name="ffn",
    )(x, g, shift, scale, w1, w3, w2, gpost, mgate)


def _rope_tables(n_lat):
    t = jnp.arange(n_lat)
    pos_row = (t // GRID_W).astype(F32)
    pos_col = (t % GRID_W).astype(F32)
    quarter = ROT_DIM // 4
    inv = ROPE_BASE ** (-jnp.arange(quarter, dtype=F32) / quarter)
    ang = jnp.concatenate([pos_row[:, None] * inv, pos_col[:, None] * inv], axis=-1)
    cos, sin = jnp.cos(ang), jnp.sin(ang)
    cosf = jnp.tile(cos, (1, 4))
    sins = jnp.tile(jnp.concatenate([-sin, sin], axis=-1), (1, 2))
    return cosf, sins


def _hyena_filter(n, lp):
    t = jnp.arange(n, dtype=F32)
    t_unit = t / (n - 1)
    bands = (HY_EMB - 1) // 2
    band_f = jnp.linspace(1e-4, bands - 1, bands, dtype=F32)
    w = 2.0 * math.pi * t / n
    feats = jnp.concatenate([t_unit[:, None], jnp.cos(w[:, None] * band_f), -jnp.sin(w[:, None] * band_f)], axis=-1)
    hid = jnp.sin(lp['hy_freq1'] * (feats @ lp['hy_w1'] + lp['hy_b1']))
    hid = jnp.sin(lp['hy_freq2'] * (hid @ lp['hy_w2'] + lp['hy_b2']))
    h = (hid @ lp['hy_w3']).reshape(n, 2, HY_ORDER, HY_WIDTH)
    deltas = jnp.abs(jnp.linspace(math.log(HY_TARGET) / HY_DECAY_LONG, math.log(HY_TARGET) / HY_DECAY_SHORT,
                                  HY_WIDTH, dtype=F32))
    h = h * jnp.exp(-t_unit[:, None] * deltas)[:, None, None, :]
    fwd, bwd = h[:, 0], h[:, 1]
    g = jnp.concatenate([fwd[:1] + bwd[:1], fwd[1:], jnp.zeros_like(fwd[:1]), jnp.flip(bwd[1:], axis=0)], axis=0)
    return g * lax.rsqrt(jnp.sum(g * g, axis=0, keepdims=True) + EPS)


def _full_spectrum(g):
    half = jnp.fft.rfft(g, axis=0)
    full = jnp.concatenate([half, jnp.conj(jnp.flip(half[1:-1], axis=0))], axis=0)
    return jnp.real(full).astype(F32), jnp.imag(full).astype(F32)


def _dft_tables(n):
    na = n // DFT_B
    dft_a = 2 * na
    ka = np.arange(dft_a)[:, None]
    a = np.arange(dft_a)[None, :]
    th = 2.0 * np.pi * (ka * a % dft_a) / dft_a
    ca, sa = np.cos(th), np.sin(th)
    kb = np.arange(DFT_B)[:, None]
    tb = 2.0 * np.pi * (kb * kb.T % DFT_B) / DFT_B
    cb, sb = np.cos(tb), np.sin(tb)
    fa = np.concatenate([ca[:, :na], -sa[:, :na]], axis=0)
    mf = np.block([[cb, sb], [-sb, cb]])
    mi = np.block([[cb, -sb], [sb, cb]])
    fc = np.concatenate([ca[:na, :], -sa[:na, :]], axis=1) / (2 * n)
    bb = np.arange(DFT_B)[:, None]
    kk = np.arange(dft_a)[None, :]
    ph = 2.0 * np.pi * (bb * kk) / (2 * n)
    tw = np.stack([np.cos(ph), -np.sin(ph)], axis=1)
    tw = np.broadcast_to(tw[..., None], (DFT_B, 2, dft_a, LANES))
    f32 = lambda m: jnp.asarray(np.ascontiguousarray(m), F32)
    return f32(fa), f32(mf), f32(mi), f32(fc), f32(tw)


def _ctx_dft_tables(n):
    k = np.arange(2 * n)[:, None]
    t = np.arange(n)[None, :]
    th = 2.0 * np.pi * (k * t % (2 * n)) / (2 * n)
    fw = np.concatenate([np.cos(th), -np.sin(th)], axis=0)
    iv = np.concatenate([np.cos(th).T, -np.sin(th).T], axis=1) / (2 * n)
    return jnp.asarray(fw, F32), jnp.asarray(iv, F32)


def kernel(x, c, ctx, c_ctx, mod_w, mod_b, norm_mix_pre, norm_mix_post, norm_ffn_pre, norm_ffn_post, w_in,
           da_lambda, da_subln, ssd_conv_w, ssd_conv_b, ssd_a_log, ssd_dt_bias, ssd_d, ssd_norm, mla_q_norm,
           mla_w_uq, mla_kv_norm, mla_w_ukv, hy_conv_w, hy_conv_b, hy_w1, hy_b1, hy_freq1, hy_w2, hy_b2, hy_freq2,
           hy_w3, hy_bias, w_br_da, w_br_ssd, w_br_mla, w_br_hy, w_out, ffn_w1, ffn_w3, ffn_w2):
    bsz, n_lat, _ = x.shape
    n_ctx = ctx.shape[1]
    depth = w_in.shape[0]
    rope_tabs = _rope_tables(n_lat)
    dft_tabs = _dft_tables(n_lat)
    ctx_fw, ctx_iv = _ctx_dft_tables(n_ctx)
    row = lambda v: v.reshape(1, -1).astype(F32)

    c_rows = jnp.concatenate([c, c_ctx[None, :], jnp.zeros((SUBLANES - bsz - 1, D_MODEL), F32)], axis=0)
    o_da, o_ssd, o_mla, o_hy = np.cumsum([COLS_DA, COLS_SSD, COLS_MLA, COLS_HY]).tolist()

    x_lat, x_ctx = x, ctx
    for l in range(depth):
        last = l == depth - 1
        lam_init = 0.8 - 0.6 * math.exp(-0.3 * l)
        mods = _modulation(c_rows, mod_w[l].astype(BF16), row(mod_b[l]))
        m_lat = [mods[:bsz, k * D_MODEL:(k + 1) * D_MODEL][:, None, :] for k in range(6)]
        m_ctx = [jnp.broadcast_to(mods[bsz, k * D_MODEL:(k + 1) * D_MODEL][None, None, :], (bsz, 1, D_MODEL))
                 for k in range(6)]

        wl = w_in[l]
        w_da = wl[:, :o_da].astype(BF16)
        w_ssd = jnp.pad(wl[:, o_da:o_ssd], ((0, 0), (0, 1664 - COLS_SSD))).astype(BF16)
        w_mla = jnp.pad(wl[:, o_ssd:o_mla], ((0, 0), (0, 768 - COLS_MLA))).astype(BF16)
        w_hy = wl[:, o_mla:o_hy].astype(BF16)
        w_gate = wl[:, o_hy:].astype(BF16)
        wuq = mla_w_uq[l].reshape(MLA_Q_LORA, MLA_HEADS, MLA_NOPE + MLA_ROPE)
        wuq = jnp.pad(wuq, ((0, 0), (0, 0), (0, 256 - MLA_NOPE - MLA_ROPE))).reshape(MLA_Q_LORA, 1024).astype(BF16)
        wukv = mla_w_ukv[l].astype(BF16)
        g_pre = row(norm_mix_pre[l])

        lv = da_lambda[l].astype(F32)
        lam = (jnp.exp(jnp.sum(lv[0] * lv[1])) - jnp.exp(jnp.sum(lv[2] * lv[3])) + lam_init).reshape(1, 1)
        da_fin = (lam, row(da_subln[l]), lam_init)

        dtb = jnp.pad(ssd_dt_bias[l].reshape(1, -1), ((0, 0), (0, LANES - 2 * SSD_HEADS))).astype(F32)
        a_row = jnp.pad(-jnp.exp(ssd_a_log[l].astype(F32)).reshape(1, -1), ((0, 0), (0, LANES - 2 * SSD_HEADS)))
        d_row = row(jnp.repeat(ssd_d[l], SSD_HEAD_DIM))
        g_ssd = row(ssd_norm[l])

        lp = {'hy_w1': hy_w1[l], 'hy_b1': hy_b1[l], 'hy_freq1': hy_freq1[l], 'hy_w2': hy_w2[l],
              'hy_b2': hy_b2[l], 'hy_freq2': hy_freq2[l], 'hy_w3': hy_w3[l]}
        hy_bias_l = hy_bias[l].astype(F32)

        need_ctx = not last
        streams = [(x_lat, m_lat, rope_tabs), (x_ctx, m_ctx, None)]
        proj = []
        for xs, ms, rt in streams:
            q_da, k_da, v_da = _inproj_da(xs, g_pre, ms[0], ms[1], w_da, rt)
            z_s, xbc, dt_raw = _inproj_ssd(xs, g_pre, ms[0], ms[1], w_ssd)
            q_m, k_m, v_m = _inproj_mla(xs, g_pre, ms[0], ms[1], w_mla, row(mla_q_norm[l]), wuq,
                                        row(mla_kv_norm[l]), wukv, rt)
            proj.append(dict(q_da=q_da, k_da=k_da, v_da=v_da, z=z_s, xbc=xbc, dt=dt_raw, q_m=q_m, k_m=k_m, v_m=v_m))
        pl_, pc_ = proj

        da_lat = _flash(pl_['q_da'], pc_['k_da'], pc_['v_da'], pl_['k_da'], pl_['v_da'],
                        ncomp=2, heads=DA_HEADS, dq=LANES, da=da_fin)
        mla_lat = _flash(pl_['q_m'], pc_['k_m'], pc_['v_m'], pl_['k_m'], pl_['v_m'],
                         ncomp=1, heads=MLA_HEADS, dq=256)
        cw, cb_ = ssd_conv_w[l].astype(F32), row(ssd_conv_b[l])
        xc_c = _dwconv(pc_['xbc'], cw, cb_, 1, True)
        xc_l = _dwconv(pl_['xbc'], cw, cb_, 1, True)
        zeros = jnp.zeros((bsz, 4, 128, 128), F32)
        yf_c, sf = _ssd_scan(xc_c, pc_['dt'], zeros, dtb, a_row, False)
        yf_l, _ = _ssd_scan(xc_l, pl_['dt'], sf, dtb, a_row, False)
        ssd_ctx, sb = _ssd_scan(xc_c, pc_['dt'], zeros, dtb, a_row, True, fin=(yf_c, pc_['z'], d_row, g_ssd))
        ssd_lat, _ = _ssd_scan(xc_l, pl_['dt'], sb, dtb, a_row, True, fin=(yf_l, pl_['z'], d_row, g_ssd))
        hw, hb = hy_conv_w[l].astype(F32), row(hy_conv_b[l])
        u_l = _dwconv(_inproj_plain(x_lat, g_pre, m_lat[0], m_lat[1], w_hy, False), hw, hb, 3, False)
        sr, si = _full_spectrum(_hyena_filter(n_lat, lp))
        to_l = lambda s: s.reshape(DFT_B, 2 * n_lat // DFT_B, HY_ORDER, HY_WIDTH).transpose(2, 1, 0, 3)
        spec_l = jnp.stack([to_l(sr), to_l(si)], axis=1)
        z1 = _hy_long_conv(u_l, 0, u_l, 1, spec_l[0], hy_bias_l[0:1], dft_tabs, F32)
        hy_lat = _hy_long_conv(z1, 0, u_l, 2, spec_l[1], hy_bias_l[1:2], dft_tabs, BF16)[0]

        gates_l = _inproj_plain(x_lat, g_pre, m_lat[0], m_lat[1], w_gate, True)
        wbr = jnp.stack([w_br_da[l], w_br_ssd[l], w_br_mla[l], w_br_hy[l]]).astype(BF16)
        wout = w_out[l].astype(BF16)
        g_post = row(norm_mix_post[l])
        w1, w3, w2 = ffn_w1[l].astype(BF16), ffn_w3[l].astype(BF16), ffn_w2[l].astype(BF16)
        gf_pre, gf_post = row(norm_ffn_pre[l]), row(norm_ffn_post[l])

        x_lat = _merge(x_lat, (da_lat, ssd_lat, mla_lat, hy_lat), gates_l, wbr, wout, g_post, m_lat[2])
        x_lat = _ffn(x_lat, gf_pre, m_lat[3], m_lat[4], w1, w3, w2, gf_post, m_lat[5])

        if need_ctx:
            da_ctx = _flash(pc_['q_da'], pc_['k_da'], pc_['v_da'], None, None,
                            ncomp=2, heads=DA_HEADS, dq=LANES, da=da_fin)
            mla_ctx = _flash(pc_['q_m'], pc_['k_m'], pc_['v_m'], None, None, ncomp=1, heads=MLA_HEADS, dq=256)
            u_c = _dwconv(_inproj_plain(x_ctx, g_pre, m_ctx[0], m_ctx[1], w_hy, False), hw, hb, 3, False)
            scr, sci = _full_spectrum(_hyena_filter(n_ctx, lp))
            spec_c = jnp.stack([scr, sci], axis=1).transpose(2, 1, 0, 3)
            hy_ctx = _hy_ctx(u_c, spec_c, ctx_fw, ctx_iv, hy_bias_l)
            gates_c = _inproj_plain(x_ctx, g_pre, m_ctx[0], m_ctx[1], w_gate, True)
            x_ctx = _merge(x_ctx, (da_ctx, ssd_ctx, mla_ctx, hy_ctx), gates_c, wbr, wout, g_post, m_ctx[2])
            x_ctx = _ffn(x_ctx, gf_pre, m_ctx[3], m_ctx[4], w1, w3, w2, gf_post, m_ctx[5])
    return x_lat
```

```python
import functools
import math

import jax
import jax.numpy as jnp
import numpy as np
from jax import lax
from jax.experimental import pallas as pl
from jax.experimental.pallas import tpu as pltpu

F32 = jnp.float32
BF16 = jnp.bfloat16
HIGHEST = lax.Precision.HIGHEST

D_MODEL = 1024
DEPTH = 2
GRID_W = 64
EPS = 1e-6
ROPE_BASE = 10000.0
ROT_DIM = 64

DA_HEADS = 4
DA_HEAD_DIM = 64
DA_V_DIM = 128
DA_QK_COLS = 512
DA_WIDTH = 512
COLS_DA = 1536

SSD_HEADS = 8
SSD_HEAD_DIM = 64
SSD_INNER = 512
SSD_GROUPS = 2
SSD_STATE = 128
SSD_CONV = 5
SSD_CHUNK = 128
SSD_XBC = 1024
COLS_SSD = 1552

MLA_HEADS = 4
MLA_Q_LORA = 384
MLA_KV_LORA = 256
MLA_NOPE = 128
MLA_ROPE = 64
MLA_V = 128
MLA_WIDTH = 512
COLS_MLA = 704

HY_WIDTH = 512
HY_ORDER = 2
HY_SHORT = 3
HY_EMB = 33
HY_TARGET = 1e-2
HY_DECAY_SHORT = 0.3
HY_DECAY_LONG = 1.5
COLS_HY = 1536

N_BRANCH = 4
COLS_GATE = 4096
FFN_HIDDEN = 2816

LANES = 128
SUBLANES = 8
VMEM_LIMIT_BYTES = 48 * 2 ** 20
NEG_BIG = -1e30
LOG2_E = 1.4426950408889634

ROW_TILE = 512
ATTN_TQ = 512
ATTN_TK = 2048
FFN_TH = 1408
DFT_B = 128


def _params(sem):
    return pltpu.CompilerParams(dimension_semantics=sem, vmem_limit_bytes=VMEM_LIMIT_BYTES)


def _full(shape):
    nd = len(shape)
    return pl.BlockSpec(shape, lambda *_: (0,) * nd)


def _dot(a, b):
    return jnp.dot(a, b, preferred_element_type=F32)


def _dot_nt(a, b):
    return lax.dot_general(a, b, (((1,), (1,)), ((), ())), preferred_element_type=F32)


def _dot_tn(a, b):
    return lax.dot_general(a, b, (((0,), (0,)), ((), ())), preferred_element_type=F32)


def _dot_hi(a, b):
    return jnp.dot(a, b, precision=HIGHEST, preferred_element_type=F32)


def _rms(x, g):
    return x * lax.rsqrt(jnp.mean(x * x, axis=-1, keepdims=True) + EPS) * g


def _norm_mod(x, g, shift, scale):
    return _rms(x, g) * (1.0 + scale) + shift


def _silu(x):
    return x * jax.nn.sigmoid(x)


def _rope128(x, cosf, sins, first_half):
    partner = jnp.where(first_half, pltpu.roll(x, 96, 1), pltpu.roll(x, 32, 1))
    return x * cosf + partner * sins


def _first_half_mask(rows):
    return (lax.broadcasted_iota(jnp.int32, (rows, LANES), 1) & 32) == 0


def _mod_kernel(c_ref, w_ref, b_ref, o_ref):
    act = _silu(c_ref[...]).astype(BF16)
    o_ref[...] = _dot(act, w_ref[...]) + b_ref[...]


def _modulation(c_rows, w, b):
    n = w.shape[1] // D_MODEL
    return pl.pallas_call(
        _mod_kernel,
        grid=(n,),
        in_specs=[_full((SUBLANES, D_MODEL)),
                  pl.BlockSpec((D_MODEL, D_MODEL), lambda j: (0, j)),
                  pl.BlockSpec((1, D_MODEL), lambda j: (0, j))],
        out_specs=pl.BlockSpec((SUBLANES, D_MODEL), lambda j: (0, j)),
        out_shape=jax.ShapeDtypeStruct((SUBLANES, w.shape[1]), F32),
        compiler_params=_params(("parallel",)),
        name="modulation",
    )(c_rows, w, b)


def _inproj_call(kernel, x, g, shift, scale, extra, extra_specs, out_dims, out_dtypes, name, extra_out=None):
    b, l, _ = x.shape
    tm = min(ROW_TILE, l)
    row = lambda d: pl.BlockSpec((1, tm, d), lambda bi, i: (bi, i, 0))
    vec = pl.BlockSpec((1, 1, D_MODEL), lambda bi, i: (bi, 0, 0))
    out_specs = [row(d) for d in out_dims]
    out_shape = [jax.ShapeDtypeStruct((b, l, d), dt) for d, dt in zip(out_dims, out_dtypes)]
    if extra_out is not None:
        out_specs.append(extra_out[0](tm))
        out_shape.append(extra_out[1])
    return pl.pallas_call(
        kernel,
        grid=(b, l // tm),
        in_specs=[row(D_MODEL), _full((1, D_MODEL)), vec, vec] + extra_specs(tm),
        out_specs=out_specs,
        out_shape=out_shape,
        compiler_params=_params(("parallel", "parallel")),
        name=name,
    )(x, g, shift, scale, *extra)


def _rope_specs(tm):
    return [pl.BlockSpec((tm, LANES), lambda bi, i: (i, 0))] * 2


def _inproj_da_kernel(*refs, rope):
    if rope:
        x_ref, g_ref, sh_ref, sc_ref, w_ref, wvt_ref, cos_ref, sin_ref, q_ref, k_ref, vt_ref = refs
    else:
        x_ref, g_ref, sh_ref, sc_ref, w_ref, wvt_ref, q_ref, k_ref, vt_ref = refs
    h = _norm_mod(x_ref[0], g_ref[...], sh_ref[0], sc_ref[0]).astype(BF16)
    tm = h.shape[0]
    first = _first_half_mask(tm)
    scale = DA_HEAD_DIM ** -0.5 * LOG2_E
    for out_ref, c0, is_q in ((q_ref, 0, True), (k_ref, 512, False)):
        res = _dot(h, w_ref[:, c0:c0 + 512])
        for i in range(4):
            t = res[:, i * LANES:(i + 1) * LANES]
            if rope:
                t = _rope128(t, cos_ref[...], sin_ref[...], first)
            if is_q:
                t = t * scale
            out_ref[0, :, i * LANES:(i + 1) * LANES] = t.astype(BF16)
    vt_ref[0] = _dot_nt(wvt_ref[...], h).astype(BF16)


def _tokens_last_spec(width, tm):
    return pl.BlockSpec((1, width, tm), lambda bi, i: (bi, 0, i))


def _inproj_da(x, g, shift, scale, w_qk, w_vt, rope_tabs):
    b, l, _ = x.shape
    rope = rope_tabs is not None
    extra = [w_qk, w_vt] + (list(rope_tabs) if rope else [])
    specs = lambda tm: [_full(w_qk.shape), _full(w_vt.shape)] + (_rope_specs(tm) if rope else [])
    return _inproj_call(functools.partial(_inproj_da_kernel, rope=rope), x, g, shift, scale, extra, specs,
                        (512, 512), (BF16, BF16), "inproj_da",
                        extra_out=(lambda tm: _tokens_last_spec(512, tm), jax.ShapeDtypeStruct((b, 512, l), BF16)))


def _inproj_mla_kernel(*refs, rope):
    if rope:
        (x_ref, g_ref, sh_ref, sc_ref, w_ref, gq_ref, wuq_ref, gkv_ref, wuk_ref, wuvt_ref, cos_ref, sin_ref,
         q_ref, k_ref, vt_ref) = refs
    else:
        (x_ref, g_ref, sh_ref, sc_ref, w_ref, gq_ref, wuq_ref, gkv_ref, wuk_ref, wuvt_ref,
         q_ref, k_ref, vt_ref) = refs
    h = _norm_mod(x_ref[0], g_ref[...], sh_ref[0], sc_ref[0]).astype(BF16)
    tm = h.shape[0]
    first = _first_half_mask(tm)
    res = _dot(h, w_ref[...])
    cq = _rms(res[:, :MLA_Q_LORA], gq_ref[...]).astype(BF16)
    ckv = _rms(res[:, MLA_Q_LORA:MLA_Q_LORA + MLA_KV_LORA], gkv_ref[...]).astype(BF16)
    kr = res[:, MLA_Q_LORA + MLA_KV_LORA:]
    q = _dot(cq, wuq_ref[...])
    kn = _dot(ckv, wuk_ref[...])
    if rope:
        kr = _rope128(kr, cos_ref[...], sin_ref[...], first)
    kr = kr.astype(BF16)
    scale = (MLA_NOPE + MLA_ROPE) ** -0.5 * LOG2_E
    for hh in range(MLA_HEADS):
        c0 = hh * 256
        qr = q[:, c0 + LANES:c0 + 256]
        if rope:
            qr = _rope128(qr, cos_ref[...], sin_ref[...], first)
        q_ref[0, :, c0:c0 + LANES] = (q[:, c0:c0 + LANES] * scale).astype(BF16)
        q_ref[0, :, c0 + LANES:c0 + 256] = (qr * scale).astype(BF16)
        k_ref[0, :, c0:c0 + LANES] = kn[:, hh * LANES:(hh + 1) * LANES].astype(BF16)
        k_ref[0, :, c0 + LANES:c0 + 256] = kr
    vt_ref[0] = _dot_nt(wuvt_ref[...], ckv).astype(BF16)


def _inproj_mla(x, g, shift, scale, w, gq, wuq, gkv, wuk, wuvt, rope_tabs):
    b, l, _ = x.shape
    rope = rope_tabs is not None
    consts = (w, gq, wuq, gkv, wuk, wuvt)
    extra = list(consts) + (list(rope_tabs) if rope else [])
    specs = lambda tm: [_full(a.shape) for a in consts] + (_rope_specs(tm) if rope else [])
    return _inproj_call(functools.partial(_inproj_mla_kernel, rope=rope), x, g, shift, scale, extra, specs,
                        (1024, 1024), (BF16, BF16), "inproj_mla",
                        extra_out=(lambda tm: _tokens_last_spec(512, tm), jax.ShapeDtypeStruct((b, 512, l), BF16)))


def _inproj_ssd_kernel(x_ref, g_ref, sh_ref, sc_ref, w_ref, z_ref, xbc_ref, dt_ref):
    h = _norm_mod(x_ref[0], g_ref[...], sh_ref[0], sc_ref[0]).astype(BF16)
    z_ref[0] = _dot(h, w_ref[:, 0:512])
    xbc_ref[0] = _dot(h, w_ref[:, 512:1536])
    dt_ref[0] = _dot(h, w_ref[:, 1536:1664])


def _inproj_ssd(x, g, shift, scale, w):
    return _inproj_call(_inproj_ssd_kernel, x, g, shift, scale, [w], lambda tm: [_full(w.shape)],
                        (512, 1024, LANES), (F32, F32, F32), "inproj_ssd")


def _inproj_plain_kernel(x_ref, g_ref, sh_ref, sc_ref, w_ref, o_ref, *, gate):
    h = _norm_mod(x_ref[0], g_ref[...], sh_ref[0], sc_ref[0]).astype(BF16)
    n = w_ref.shape[1]
    step = 512
    for c0 in range(0, n, step):
        res = _dot(h, w_ref[:, c0:c0 + step])
        if gate:
            res = jax.nn.sigmoid(res)
        o_ref[0, :, c0:c0 + step] = res.astype(o_ref.dtype)


def _inproj_plain(x, g, shift, scale, w, gate):
    return _inproj_call(functools.partial(_inproj_plain_kernel, gate=gate), x, g, shift, scale, [w],
                        lambda tm: [_full(w.shape)], (w.shape[1],), (BF16 if gate else F32,),
                        "inproj_gate" if gate else "inproj_hy")[0]


def _flash_kernel(*refs, ncomp, has_lat, da_finish, lam_init):
    q_ref, kc_ref, vc_ref = refs[:3]
    pos = 3
    if has_lat:
        kl_ref, vl_ref = refs[3:5]
        pos = 5
    if da_finish:
        lam_ref, sub_ref = refs[pos:pos + 2]
        pos += 2
    o_ref, m_ref, l_ref, acc_ref = refs[pos:pos + 4]
    j = pl.program_id(3)
    nk = pl.num_programs(3)
    q = q_ref[0]
    tq = q.shape[0]
    if ncomp == 2:
        lo = lax.broadcasted_iota(jnp.int32, (tq, LANES), 1) < DA_HEAD_DIM
        zero = jnp.zeros_like(q)
        qs = (jnp.where(lo, q, zero), jnp.where(lo, zero, q))
    else:
        qs = (q,)

    def attend(k, vt):
        for c in range(ncomp):
            st = _dot_nt(k, qs[c])
            m_prev = m_ref[c]
            m_new = jnp.maximum(m_prev, jnp.max(st, axis=0, keepdims=True))
            alpha = jnp.exp2(m_prev - m_new)
            p = jnp.exp2(st - m_new)
            l_ref[c] = alpha * l_ref[c] + jnp.sum(p, axis=0, keepdims=True)
            acc_ref[c] = alpha * acc_ref[c] + _dot(vt, p.astype(BF16))
            m_ref[c] = m_new

    @pl.when(j == 0)
    def _():
        m_ref[...] = jnp.full(m_ref.shape, NEG_BIG, F32)
        l_ref[...] = jnp.zeros(l_ref.shape, F32)
        acc_ref[...] = jnp.zeros(acc_ref.shape, F32)
        attend(kc_ref[0], vc_ref[0])

    if has_lat:
        attend(kl_ref[0], vl_ref[0])

    @pl.when(j == nk - 1)
    def _():
        o = acc_ref[0] / l_ref[0]
        if da_finish:
            o = o - lam_ref[...] * (acc_ref[1] / l_ref[1])
            ms = jnp.mean(o * o, axis=0, keepdims=True)
            o = o * lax.rsqrt(ms + EPS) * sub_ref[...] * (1.0 - lam_init)
        o_ref[0] = o.T.astype(o_ref.dtype)


def _flash(q, k_ctx, vt_ctx, k_lat, vt_lat, *, ncomp, heads, dq, da=None):
    b, lq, _ = q.shape
    lc = k_ctx.shape[1]
    has_lat = k_lat is not None
    tq = min(ATTN_TQ, lq)
    nq = lq // tq
    if has_lat:
        tk = min(ATTN_TK, k_lat.shape[1])
        nk = k_lat.shape[1] // tk
    else:
        nk = 1
    in_specs = [pl.BlockSpec((1, tq, dq), lambda bi, h, i, j: (bi, i, h)),
                pl.BlockSpec((1, lc, dq), lambda bi, h, i, j: (bi, 0, h)),
                pl.BlockSpec((1, LANES, lc), lambda bi, h, i, j: (bi, h, 0))]
    args = [q, k_ctx, vt_ctx]
    if has_lat:
        in_specs += [pl.BlockSpec((1, tk, dq), lambda bi, h, i, j: (bi, j, h)),
                     pl.BlockSpec((1, LANES, tk), lambda bi, h, i, j: (bi, h, j))]
        args += [k_lat, vt_lat]
    lam_init = 0.0
    if da is not None:
        lam, subln, lam_init = da
        in_specs += [_full((1, 1)), _full((LANES, 1))]
        args += [lam, subln]
    kernel = functools.partial(_flash_kernel, ncomp=ncomp, has_lat=has_lat, da_finish=da is not None,
                               lam_init=lam_init)
    return pl.pallas_call(
        kernel,
        grid=(b, heads, nq, nk),
        in_specs=in_specs,
        out_specs=pl.BlockSpec((1, tq, LANES), lambda bi, h, i, j: (bi, i, h)),
        out_shape=jax.ShapeDtypeStruct((b, lq, heads * LANES), BF16),
        scratch_shapes=[pltpu.VMEM((ncomp, 1, tq), F32), pltpu.VMEM((ncomp, 1, tq), F32),
                        pltpu.VMEM((ncomp, LANES, tq), F32)],
        compiler_params=_params(("parallel", "parallel", "parallel", "arbitrary")),
        name="flash_da" if ncomp == 2 else "flash_mla",
    )(*args)


def _dwconv_kernel(x_ref, w_ref, b_ref, o_ref, pad_ref, *, taps, act, chunk):
    l = x_ref.shape[1]
    pad_ref[0:SUBLANES, :] = jnp.zeros((SUBLANES, LANES), F32)
    pad_ref[SUBLANES + l:2 * SUBLANES + l, :] = jnp.zeros((SUBLANES, LANES), F32)
    pad_ref[SUBLANES:SUBLANES + l, :] = x_ref[0]
    w = w_ref[...]
    bias = b_ref[...]

    def body(i, carry):
        base = pl.multiple_of(i * chunk, chunk)
        acc = jnp.zeros((chunk, LANES), F32) + bias
        for t in range(taps):
            acc = acc + w[t:t + 1, :] * pad_ref[pl.ds(base + SUBLANES + t - taps // 2, chunk), :]
        if act:
            acc = _silu(acc)
        o_ref[0, 0, pl.ds(base, chunk), :] = acc
        return carry

    lax.fori_loop(0, l // chunk, body, 0)


def _dwconv(x, w, bias, groups, act):
    b, l, c = x.shape
    taps = w.shape[0]
    cg = c // groups
    per = cg // LANES
    chunk = min(512, l)
    return pl.pallas_call(
        functools.partial(_dwconv_kernel, taps=taps, act=act, chunk=chunk),
        grid=(b, c // LANES),
        in_specs=[pl.BlockSpec((1, l, LANES), lambda bi, ct: (bi, 0, ct)),
                  pl.BlockSpec((taps, LANES), lambda bi, ct: (0, ct)),
                  pl.BlockSpec((1, LANES), lambda bi, ct: (0, ct))],
        out_specs=pl.BlockSpec((1, 1, l, LANES), lambda bi, ct: (ct // per, bi, 0, ct % per)),
        out_shape=jax.ShapeDtypeStruct((groups, b, l, cg), F32),
        scratch_shapes=[pltpu.VMEM((l + 2 * SUBLANES, LANES), F32)],
        compiler_params=_params(("parallel", "parallel")),
        name="dwconv",
    )(x, w, bias)


def _ssd_kernel(*refs, reverse, finish):
    if finish:
        (xc_ref, dt_ref, init_ref, dtb_ref, a_ref, yo_ref, z_ref, d_ref, g_ref,
         y_ref, fin_ref, st_ref) = refs
    else:
        xc_ref, dt_ref, init_ref, dtb_ref, a_ref, y_ref, fin_ref, st_ref = refs
    j = pl.program_id(1)
    nc = pl.num_programs(1)
    q = SSD_CHUNK

    @pl.when(j == 0)
    def _():
        st_ref[...] = init_ref[0]

    xc = xc_ref[0, 0]
    pre = dt_ref[0] + dtb_ref[...]
    dt = jnp.maximum(pre, 0.0) + jnp.log1p(jnp.exp(-jnp.abs(pre)))
    a = dt * a_ref[...]
    r = lax.broadcasted_iota(jnp.int32, (q, q), 0)
    c = lax.broadcasted_iota(jnp.int32, (q, q), 1)
    tri = (c >= r) if reverse else (c <= r)
    cum = _dot_hi(tri.astype(F32), a)
    cum_t = cum.T
    tot = cum[0:1, :] if reverse else cum[q - 1:q, :]
    off = SSD_HEADS if reverse else 0
    lane_lo = c < SSD_HEAD_DIM
    row_lo = r < SSD_HEAD_DIM
    ys = []
    for g in range(SSD_GROUPS):
        bm = xc[:, 512 + g * 128:512 + (g + 1) * 128].astype(BF16)
        cm = xc[:, 768 + g * 128:768 + (g + 1) * 128].astype(BF16)
        cb = _dot_nt(cm, bm)
        for pp in range(2):
            hp = 2 * g + pp
            l0 = off + 2 * hp
            l1 = l0 + 1
            col0, col1 = cum[:, l0:l0 + 1], cum[:, l1:l1 + 1]
            dec0 = jnp.exp(jnp.where(tri, col0 - cum_t[l0:l0 + 1, :], NEG_BIG))
            dec1 = jnp.exp(jnp.where(tri, col1 - cum_t[l1:l1 + 1, :], NEG_BIG))
            xdt = xc[:, hp * 128:(hp + 1) * 128] * jnp.where(lane_lo, dt[:, l0:l0 + 1], dt[:, l1:l1 + 1])
            xdt_b = xdt.astype(BF16)
            zero = jnp.zeros_like(xdt_b)
            y = (_dot((cb * dec0).astype(BF16), jnp.where(lane_lo, xdt_b, zero))
                 + _dot((cb * dec1).astype(BF16), jnp.where(lane_lo, zero, xdt_b)))
            colp = jnp.where(lane_lo, col0, col1)
            totp = jnp.where(lane_lo, tot[:, l0:l0 + 1], tot[:, l1:l1 + 1])
            contrib = _dot_tn((xdt * jnp.exp(totp - colp)).astype(BF16), bm)
            prev = st_ref[hp]
            y = y + _dot_nt(cm, prev.astype(BF16)) * jnp.exp(colp)
            st_ref[hp] = prev * jnp.exp(jnp.where(row_lo, tot[:, l0:l0 + 1], tot[:, l1:l1 + 1])) + contrib
            if finish:
                ys.append(y)
            else:
                y_ref[0, :, hp * 128:(hp + 1) * 128] = y

    if finish:
        yt = jnp.concatenate(ys, axis=1) + yo_ref[0] + d_ref[...] * xc[:, :SSD_INNER]
        y_ref[0] = _rms(yt * _silu(z_ref[0]), g_ref[...]).astype(y_ref.dtype)

    @pl.when(j == nc - 1)
    def _():
        fin_ref[0] = st_ref[...]


def _ssd_scan(xc, dt_raw, init, dtb, a_row, reverse, fin=None):
    _, b, l, _ = xc.shape
    nc = l // SSD_CHUNK
    cidx = (lambda j: nc - 1 - j) if reverse else (lambda j: j)
    rows = lambda d: pl.BlockSpec((1, SSD_CHUNK, d), lambda bi, j: (bi, cidx(j), 0))
    st_spec = pl.BlockSpec((1, 4, 128, 128), lambda bi, j: (bi, 0, 0, 0))
    in_specs = [pl.BlockSpec((1, 1, SSD_CHUNK, SSD_XBC), lambda bi, j: (0, bi, cidx(j), 0)),
                rows(LANES), st_spec, _full((1, LANES)), _full((1, LANES))]
    args = [xc, dt_raw, init, dtb, a_row]
    finish = fin is not None
    if finish:
        y_other, z, d_row, g = fin
        in_specs += [rows(SSD_INNER), rows(SSD_INNER), _full((1, SSD_INNER)), _full((1, SSD_INNER))]
        args += [y_other, z, d_row, g]
    return pl.pallas_call(
        functools.partial(_ssd_kernel, reverse=reverse, finish=finish),
        grid=(b, nc),
        in_specs=in_specs,
        out_specs=[rows(SSD_INNER), st_spec],
        out_shape=[jax.ShapeDtypeStruct((b, l, SSD_INNER), BF16 if finish else F32),
                   jax.ShapeDtypeStruct((b, 4, 128, 128), F32)],
        scratch_shapes=[pltpu.VMEM((4, 128, 128), F32)],
        compiler_params=_params(("parallel", "arbitrary")),
        name="ssd_bwd" if reverse else "ssd_fwd",
    )(*args)


def _split_bf16(x):
    hi = x.astype(BF16)
    return hi, (x - hi.astype(F32)).astype(BF16)


def _dot_split(a3_ref, b):
    b_hi, b_lo = _split_bf16(b)
    return _dot(a3_ref[...], jnp.concatenate([b_hi, b_hi, b_lo], axis=0))


def _hy_dft_a_kernel(z_ref, fa_ref, tw_ref, o_ref):
    z = z_ref[0, 0]
    res = _dot_split(fa_ref, z)
    na2 = fa_ref.shape[0] // 2
    xr, xi = res[:na2], res[na2:]
    reps = z.shape[1] // LANES
    tr = jnp.concatenate([tw_ref[0, 0]] * reps, axis=1)
    ti = jnp.concatenate([tw_ref[0, 1]] * reps, axis=1)
    o_ref[0, 0] = xr * tr - xi * ti
    o_ref[0, 1] = xr * ti + xi * tr


def _hy_dft_a(zv, group, fa3, tw):
    _, b, rows, cols = zv.shape
    c = cols // DFT_B
    dft_a = tw.shape[2]
    return pl.pallas_call(
        _hy_dft_a_kernel,
        grid=(b, DFT_B),
        in_specs=[pl.BlockSpec((1, 1, rows, c), lambda bi, jb: (group, bi, 0, jb)),
                  _full(fa3.shape),
                  pl.BlockSpec((1, 2, dft_a, LANES), lambda bi, jb: (jb, 0, 0, 0))],
        out_specs=pl.BlockSpec((1, 2, dft_a, c), lambda bi, jb: (bi, 0, 0, jb)),
        out_shape=jax.ShapeDtypeStruct((b, 2, dft_a, cols), F32),
        compiler_params=_params(("parallel", "parallel")),
        name="hyena_dft_a",
    )(zv, fa3, tw)


def _hy_dft_b_kernel(*refs, conv):
    if conv:
        y_ref, s_ref, mf_ref, mi_ref, o_ref = refs
    else:
        y_ref, mf_ref, o_ref = refs
    y = jnp.concatenate([y_ref[0, 0, 0], y_ref[0, 1, 0]], axis=0)
    out = _dot_split(mf_ref, y)
    if conv:
        xr, xi = out[:DFT_B], out[DFT_B:]
        sr, si = s_ref[0, 0], s_ref[1, 0]
        out = _dot_split(mi_ref, jnp.concatenate([xr * sr - xi * si, xr * si + xi * sr], axis=0))
    o_ref[0, 0, 0] = out[:DFT_B]
    o_ref[0, 1, 0] = out[DFT_B:]


def _hy_dft_b(y1, mf3, spec_mi=None):
    b, _, dft_a, _, c = y1.shape
    conv = spec_mi is not None
    blk = pl.BlockSpec((1, 2, 1, DFT_B, c), lambda ka, bi: (bi, 0, ka, 0, 0))
    in_specs, args = [blk], [y1]
    if conv:
        in_specs += [pl.BlockSpec((2, 1, DFT_B, c), lambda ka, bi: (0, ka, 0, 0)), _full(mf3.shape),
                     _full(spec_mi[1].shape)]
        args += [spec_mi[0], mf3, spec_mi[1]]
    else:
        in_specs += [_full(mf3.shape)]
        args += [mf3]
    return pl.pallas_call(
        functools.partial(_hy_dft_b_kernel, conv=conv),
        grid=(dft_a, b),
        in_specs=in_specs,
        out_specs=blk,
        out_shape=jax.ShapeDtypeStruct(y1.shape, F32),
        compiler_params=_params(("parallel", "parallel")),
        name="hyena_dft_b" if conv else "hyena_spectrum_b",
    )(*args)


def _hy_dft_c_kernel(y_ref, tw_ref, fc_ref, gate_ref, zp_ref, bias_ref, o_ref):
    yr, yi = y_ref[0, 0], y_ref[0, 1]
    reps = yr.shape[1] // LANES
    tr = jnp.concatenate([tw_ref[0, 0]] * reps, axis=1)
    ti = jnp.concatenate([tw_ref[0, 1]] * reps, axis=1)
    stacked = jnp.concatenate([yr * tr + yi * ti, yi * tr - yr * ti], axis=0)
    conv = _dot_split(fc_ref, stacked)
    zp = zp_ref[0, 0]
    o_ref[0] = (gate_ref[0, 0] * (conv + bias_ref[...] * zp)).astype(o_ref.dtype)


def _hy_spectrum(g, tabs):
    o, n2, c = g.shape
    dft_a = n2 // DFT_B
    y1 = _hy_dft_a(g.reshape(1, o, dft_a, DFT_B * c), 0, tabs['fa_full'], tabs['tw'])
    return _hy_dft_b(y1.reshape(o, 2, dft_a, DFT_B, c), tabs['mf'])


def _hy_long_conv(zsrc, zgroup, u4, gate_group, spec_l, bias_row, tabs, out_dtype):
    fc, tw = tabs['fc'], tabs['tw']
    gz, b, n, c = zsrc.shape
    na = n // DFT_B
    dft_a = 2 * na
    cols = DFT_B * c
    zv = zsrc.reshape(gz, b, na, cols)
    y1 = _hy_dft_a(zv, zgroup, tabs['fa'], tw)
    y2 = _hy_dft_b(y1.reshape(b, 2, dft_a, DFT_B, c), tabs['mf'], (spec_l, tabs['mi']))
    gv = u4.reshape(u4.shape[0], b, na, cols)
    out = pl.pallas_call(
        _hy_dft_c_kernel,
        grid=(b, DFT_B),
        in_specs=[pl.BlockSpec((1, 2, dft_a, c), lambda bi, jb: (bi, 0, 0, jb)),
                  pl.BlockSpec((1, 2, dft_a, LANES), lambda bi, jb: (jb, 0, 0, 0)),
                  _full(fc.shape),
                  pl.BlockSpec((1, 1, na, c), lambda bi, jb: (gate_group, bi, 0, jb)),
                  pl.BlockSpec((1, 1, na, c), lambda bi, jb: (zgroup, bi, 0, jb)),
                  _full((1, c))],
        out_specs=pl.BlockSpec((1, na, c), lambda bi, jb: (bi, 0, jb)),
        out_shape=jax.ShapeDtypeStruct((b, na, cols), out_dtype),
        compiler_params=_params(("parallel", "parallel")),
        name="hyena_dft_c",
    )(y2.reshape(b, 2, dft_a, cols), tw, fc, gv, zv, bias_row)
    return out.reshape(1, b, n, c)


def _hy_ctx_kernel(v_ref, x1_ref, x2_ref, s_ref, fw_ref, iv_ref, bias_ref, o_ref):
    z = v_ref[0, 0]
    n2 = fw_ref.shape[0] // 2
    for o, gate_ref in enumerate((x1_ref, x2_ref)):
        x = _dot_split(fw_ref, z)
        xr, xi = x[:n2], x[n2:]
        sr, si = s_ref[0, :, o * HY_WIDTH:(o + 1) * HY_WIDTH], s_ref[1, :, o * HY_WIDTH:(o + 1) * HY_WIDTH]
        prod = jnp.concatenate([xr * sr - xi * si, xr * si + xi * sr], axis=0)
        conv = _dot_split(iv_ref, prod)
        z = gate_ref[0, 0] * (conv + bias_ref[o:o + 1, :] * z)
    o_ref[0] = z.astype(o_ref.dtype)


def _ctx_spectrum_kernel(g_ref, f_ref, o_ref):
    res = _dot_split(f_ref, g_ref[...])
    half = res.shape[0] // 2
    o_ref[0] = res[:half]
    o_ref[1] = res[half:]


def _ctx_spectrum(g, f3):
    n2, c = g.shape
    return pl.pallas_call(
        _ctx_spectrum_kernel,
        in_specs=[_full(g.shape), _full(f3.shape)],
        out_specs=_full((2, n2, c)),
        out_shape=jax.ShapeDtypeStruct((2, n2, c), F32),
        compiler_params=pltpu.CompilerParams(vmem_limit_bytes=VMEM_LIMIT_BYTES),
        name="hyena_ctx_spectrum",
    )(g, f3)


def _hy_ctx(u4, spec_c, fw, iv, bias):
    _, b, n, c = u4.shape
    grp = lambda g: pl.BlockSpec((1, 1, n, c), lambda bi: (g, bi, 0, 0))
    return pl.pallas_call(
        _hy_ctx_kernel,
        grid=(b,),
        in_specs=[grp(0), grp(1), grp(2), _full(spec_c.shape), _full(fw.shape), _full(iv.shape), _full(bias.shape)],
        out_specs=pl.BlockSpec((1, n, c), lambda bi: (bi, 0, 0)),
        out_shape=jax.ShapeDtypeStruct((b, n, c), BF16),
        compiler_params=_params(("parallel",)),
        name="hyena_ctx",
    )(u4, u4, u4, spec_c, fw, iv, bias)


def _merge_kernel(x_ref, da_ref, ssd_ref, mla_ref, hy_ref, gate_ref, wbr_ref, wout_ref, gpost_ref, mg_ref, o_ref):
    mixed = None
    for k, br_ref in enumerate((da_ref, ssd_ref, mla_ref, hy_ref)):
        t = gate_ref[0, :, k * D_MODEL:(k + 1) * D_MODEL].astype(F32) * _dot(br_ref[0], wbr_ref[k])
        mixed = t if mixed is None else mixed + t
    y = _dot(mixed.astype(BF16), wout_ref[...])
    o_ref[0] = x_ref[0] + mg_ref[0] * _rms(y, gpost_ref[...])


def _merge(x, branches, gates, wbr, wout, gpost, mgate):
    b, l, _ = x.shape
    tm = min(ROW_TILE, l)
    row = lambda d: pl.BlockSpec((1, tm, d), lambda bi, i: (bi, i, 0))
    vec = pl.BlockSpec((1, 1, D_MODEL), lambda bi, i: (bi, 0, 0))
    return pl.pallas_call(
        _merge_kernel,
        grid=(b, l // tm),
        in_specs=[row(D_MODEL)] + [row(512)] * 4 + [row(COLS_GATE), _full(wbr.shape), _full(wout.shape),
                                                    _full((1, D_MODEL)), vec],
        out_specs=row(D_MODEL),
        out_shape=jax.ShapeDtypeStruct((b, l, D_MODEL), F32),
        compiler_params=_params(("parallel", "parallel")),
        name="merge",
    )(x, *branches, gates, wbr, wout, gpost, mgate)


def _ffn_kernel(x_ref, g_ref, sh_ref, sc_ref, w1_ref, w3_ref, w2_ref, gpost_ref, mg_ref, o_ref, h_ref, acc_ref):
    k = pl.program_id(2)

    @pl.when(k == 0)
    def _():
        h_ref[...] = _norm_mod(x_ref[0], g_ref[...], sh_ref[0], sc_ref[0]).astype(BF16)
        acc_ref[...] = jnp.zeros(acc_ref.shape, F32)

    h = h_ref[...]
    t = (_silu(_dot(h, w1_ref[...])) * _dot(h, w3_ref[...])).astype(BF16)
    acc_ref[...] += _dot(t, w2_ref[...])

    @pl.when(k == pl.num_programs(2) - 1)
    def _():
        o_ref[0] = x_ref[0] + mg_ref[0] * _rms(acc_ref[...], gpost_ref[...])


def _ffn(x, g, shift, scale, w1, w3, w2, gpost, mgate):
    b, l, _ = x.shape
    tm = min(ROW_TILE, l)
    nh = FFN_HIDDEN // FFN_TH
    row = pl.BlockSpec((1, tm, D_MODEL), lambda bi, i, k: (bi, i, 0))
    vec = pl.BlockSpec((1, 1, D_MODEL), lambda bi, i, k: (bi, 0, 0))
    return pl.pallas_call(
        _ffn_kernel,
        grid=(b, l // tm, nh),
        in_specs=[row, _full((1, D_MODEL)), vec, vec,
                  pl.BlockSpec((D_MODEL, FFN_TH), lambda bi, i, k: (0, k)),
                  pl.BlockSpec((D_MODEL, FFN_TH), lambda bi, i, k: (0, k)),
                  pl.BlockSpec((FFN_TH, D_MODEL), lambda bi, i, k: (k, 0)),
                  _full((1, D_MODEL)), vec],
        out_specs=row,
        out_shape=jax.ShapeDtypeStruct((b, l, D_MODEL), F32),
        scratch_shapes=[pltpu.VMEM((tm, D_MODEL), BF16), pltpu.VMEM((tm, D_MODEL), F32)],
        compiler_params=_params(("parallel", "parallel", "arbitrary")),
        name="ffn",
    )(x, g, shift, scale, w1, w3, w2, gpost, mgate)


def _rope_tables(n_lat):
    t = jnp.arange(n_lat)
    pos_row = (t // GRID_W).astype(F32)
    pos_col = (t % GRID_W).astype(F32)
    quarter = ROT_DIM // 4
    inv = ROPE_BASE ** (-jnp.arange(quarter, dtype=F32) / quarter)
    ang = jnp.concatenate([pos_row[:, None] * inv, pos_col[:, None] * inv], axis=-1)
    cos, sin = jnp.cos(ang), jnp.sin(ang)
    cosf = jnp.tile(cos, (1, 4))
    sins = jnp.tile(jnp.concatenate([-sin, sin], axis=-1), (1, 2))
    return cosf, sins


def _hyena_filter(n, lp):
    t = jnp.arange(n, dtype=F32)
    t_unit = t / (n - 1)
    bands = (HY_EMB - 1) // 2
    band_f = jnp.linspace(1e-4, bands - 1, bands, dtype=F32)
    w = 2.0 * math.pi * t / n
    feats = jnp.concatenate([t_unit[:, None], jnp.cos(w[:, None] * band_f), -jnp.sin(w[:, None] * band_f)], axis=-1)
    hid = jnp.sin(lp['hy_freq1'] * (feats @ lp['hy_w1'] + lp['hy_b1']))
    hid = jnp.sin(lp['hy_freq2'] * (hid @ lp['hy_w2'] + lp['hy_b2']))
    hid_rev = jnp.flip(hid, axis=0)
    w3 = lp['hy_w3'].reshape(-1, 2, HY_ORDER, HY_WIDTH)
    deltas = jnp.abs(jnp.linspace(math.log(HY_TARGET) / HY_DECAY_LONG, math.log(HY_TARGET) / HY_DECAY_SHORT,
                                  HY_WIDTH, dtype=F32))
    decay = jnp.exp(-t_unit[:, None] * deltas)
    decay_rev = jnp.exp(-jnp.flip(t_unit)[:, None] * deltas)
    orders = []
    for o in range(HY_ORDER):
        fwd = (hid @ w3[:, 0, o]) * decay
        bwd_rev = (hid_rev @ w3[:, 1, o]) * decay_rev
        orders.append(jnp.concatenate([fwd[:1] + bwd_rev[n - 1:], fwd[1:], jnp.zeros_like(fwd[:1]),
                                       bwd_rev[:n - 1]], axis=0))
    g = jnp.stack(orders)
    return g * lax.rsqrt(jnp.sum(g * g, axis=1, keepdims=True) + EPS)


def _stack3(m):
    m32 = jnp.asarray(np.ascontiguousarray(m), F32)
    hi = m32.astype(BF16)
    lo = (m32 - hi.astype(F32)).astype(BF16)
    return jnp.concatenate([hi, lo, hi], axis=1)


def _dft_tables(n):
    na = n // DFT_B
    dft_a = 2 * na
    ka = np.arange(dft_a)[:, None]
    a = np.arange(dft_a)[None, :]
    th = 2.0 * np.pi * (ka * a % dft_a) / dft_a
    ca, sa = np.cos(th), np.sin(th)
    kb = np.arange(DFT_B)[:, None]
    tb = 2.0 * np.pi * (kb * kb.T % DFT_B) / DFT_B
    cb, sb = np.cos(tb), np.sin(tb)
    fa_full = np.concatenate([ca, -sa], axis=0)
    bb = np.arange(DFT_B)[:, None]
    kk = np.arange(dft_a)[None, :]
    ph = 2.0 * np.pi * (bb * kk) / (2 * n)
    tw = np.stack([np.cos(ph), -np.sin(ph)], axis=1)
    tw = np.broadcast_to(tw[..., None], (DFT_B, 2, dft_a, LANES))
    return dict(
        fa=_stack3(fa_full[:, :na]),
        fa_full=_stack3(fa_full),
        mf=_stack3(np.block([[cb, sb], [-sb, cb]])),
        mi=_stack3(np.block([[cb, -sb], [sb, cb]])),
        fc=_stack3(np.concatenate([ca[:na, :], -sa[:na, :]], axis=1) / (2 * n)),
        tw=jnp.asarray(np.ascontiguousarray(tw), F32))


def _ctx_dft_tables(n):
    k = np.arange(2 * n)[:, None]
    t = np.arange(2 * n)[None, :]
    th = 2.0 * np.pi * (k * t % (2 * n)) / (2 * n)
    fw_full = np.concatenate([np.cos(th), -np.sin(th)], axis=0)
    iv = np.concatenate([np.cos(th)[:n], -np.sin(th)[:n]], axis=1) / (2 * n)
    return dict(fw=_stack3(fw_full[:, :n]), fw_full=_stack3(fw_full), iv=_stack3(iv))


def kernel(x, c, ctx, c_ctx, mod_w, mod_b, norm_mix_pre, norm_mix_post, norm_ffn_pre, norm_ffn_post, w_in,
           da_lambda, da_subln, ssd_conv_w, ssd_conv_b, ssd_a_log, ssd_dt_bias, ssd_d, ssd_norm, mla_q_norm,
           mla_w_uq, mla_kv_norm, mla_w_ukv, hy_conv_w, hy_conv_b, hy_w1, hy_b1, hy_freq1, hy_w2, hy_b2, hy_freq2,
           hy_w3, hy_bias, w_br_da, w_br_ssd, w_br_mla, w_br_hy, w_out, ffn_w1, ffn_w3, ffn_w2):
    bsz, n_lat, _ = x.shape
    n_ctx = ctx.shape[1]
    depth = w_in.shape[0]
    rope_tabs = _rope_tables(n_lat)
    dft_tabs = _dft_tables(n_lat)
    ctx_tabs = _ctx_dft_tables(n_ctx)
    row = lambda v: v.reshape(1, -1).astype(F32)

    c_rows = jnp.concatenate([c, c_ctx[None, :], jnp.zeros((SUBLANES - bsz - 1, D_MODEL), F32)], axis=0)
    o_da, o_ssd, o_mla, o_hy = np.cumsum([COLS_DA, COLS_SSD, COLS_MLA, COLS_HY]).tolist()

    x_lat, x_ctx = x, ctx
    for l in range(depth):
        last = l == depth - 1
        lam_init = 0.8 - 0.6 * math.exp(-0.3 * l)
        mods = _modulation(c_rows, mod_w[l].astype(BF16), row(mod_b[l]))
        m_lat = [mods[:bsz, k * D_MODEL:(k + 1) * D_MODEL][:, None, :] for k in range(6)]
        m_ctx = [jnp.broadcast_to(mods[bsz, k * D_MODEL:(k + 1) * D_MODEL][None, None, :], (bsz, 1, D_MODEL))
                 for k in range(6)]

        wl = w_in[l]
        w_da_qk = wl[:, :2 * DA_QK_COLS].astype(BF16)
        w_da_vt = wl[:, 2 * DA_QK_COLS:o_da].T.astype(BF16)
        w_ssd = jnp.pad(wl[:, o_da:o_ssd], ((0, 0), (0, 1664 - COLS_SSD))).astype(BF16)
        w_mla = jnp.pad(wl[:, o_ssd:o_mla], ((0, 0), (0, 768 - COLS_MLA))).astype(BF16)
        w_hy = wl[:, o_mla:o_hy].astype(BF16)
        w_gate = wl[:, o_hy:].astype(BF16)
        wuq = mla_w_uq[l].reshape(MLA_Q_LORA, MLA_HEADS, MLA_NOPE + MLA_ROPE)
        wuq = jnp.pad(wuq, ((0, 0), (0, 0), (0, 256 - MLA_NOPE - MLA_ROPE))).reshape(MLA_Q_LORA, 1024).astype(BF16)
        wukv = mla_w_ukv[l].reshape(MLA_KV_LORA, MLA_HEADS, MLA_NOPE + MLA_V)
        wuk = wukv[:, :, :MLA_NOPE].reshape(MLA_KV_LORA, MLA_HEADS * MLA_NOPE).astype(BF16)
        wuvt = wukv[:, :, MLA_NOPE:].reshape(MLA_KV_LORA, MLA_WIDTH).T.astype(BF16)
        g_pre = row(norm_mix_pre[l])

        lv = da_lambda[l].astype(F32)
        lam = (jnp.exp(jnp.sum(lv[0] * lv[1])) - jnp.exp(jnp.sum(lv[2] * lv[3])) + lam_init).reshape(1, 1)
        da_fin = (lam, da_subln[l].reshape(-1, 1).astype(F32), lam_init)

        dtb = jnp.pad(ssd_dt_bias[l].reshape(1, -1), ((0, 0), (0, LANES - 2 * SSD_HEADS))).astype(F32)
        a_row = jnp.pad(-jnp.exp(ssd_a_log[l].astype(F32)).reshape(1, -1), ((0, 0), (0, LANES - 2 * SSD_HEADS)))
        d_row = row(jnp.repeat(ssd_d[l], SSD_HEAD_DIM))
        g_ssd = row(ssd_norm[l])

        lp = {'hy_w1': hy_w1[l], 'hy_b1': hy_b1[l], 'hy_freq1': hy_freq1[l], 'hy_w2': hy_w2[l],
              'hy_b2': hy_b2[l], 'hy_freq2': hy_freq2[l], 'hy_w3': hy_w3[l]}
        hy_bias_l = hy_bias[l].astype(F32)

        need_ctx = not last
        streams = [(x_lat, m_lat, rope_tabs), (x_ctx, m_ctx, None)]
        proj = []
        for xs, ms, rt in streams:
            q_da, k_da, v_da = _inproj_da(xs, g_pre, ms[0], ms[1], w_da_qk, w_da_vt, rt)
            z_s, xbc, dt_raw = _inproj_ssd(xs, g_pre, ms[0], ms[1], w_ssd)
            q_m, k_m, v_m = _inproj_mla(xs, g_pre, ms[0], ms[1], w_mla, row(mla_q_norm[l]), wuq,
                                        row(mla_kv_norm[l]), wuk, wuvt, rt)
            proj.append(dict(q_da=q_da, k_da=k_da, v_da=v_da, z=z_s, xbc=xbc, dt=dt_raw, q_m=q_m, k_m=k_m, v_m=v_m))
        pl_, pc_ = proj

        da_lat = _flash(pl_['q_da'], pc_['k_da'], pc_['v_da'], pl_['k_da'], pl_['v_da'],
                        ncomp=2, heads=DA_HEADS, dq=LANES, da=da_fin)
        mla_lat = _flash(pl_['q_m'], pc_['k_m'], pc_['v_m'], pl_['k_m'], pl_['v_m'],
                         ncomp=1, heads=MLA_HEADS, dq=256)
        cw, cb_ = ssd_conv_w[l].astype(F32), row(ssd_conv_b[l])
        xc_c = _dwconv(pc_['xbc'], cw, cb_, 1, True)
        xc_l = _dwconv(pl_['xbc'], cw, cb_, 1, True)
        zeros = jnp.zeros((bsz, 4, 128, 128), F32)
        yf_c, sf = _ssd_scan(xc_c, pc_['dt'], zeros, dtb, a_row, False)
        yf_l, _ = _ssd_scan(xc_l, pl_['dt'], sf, dtb, a_row, False)
        ssd_ctx, sb = _ssd_scan(xc_c, pc_['dt'], zeros, dtb, a_row, True, fin=(yf_c, pc_['z'], d_row, g_ssd))
        ssd_lat, _ = _ssd_scan(xc_l, pl_['dt'], sb, dtb, a_row, True, fin=(yf_l, pl_['z'], d_row, g_ssd))
        hw, hb = hy_conv_w[l].astype(F32), row(hy_conv_b[l])
        u_l = _dwconv(_inproj_plain(x_lat, g_pre, m_lat[0], m_lat[1], w_hy, False), hw, hb, 3, False)
        spec_l = _hy_spectrum(_hyena_filter(n_lat, lp), dft_tabs)
        z1 = _hy_long_conv(u_l, 0, u_l, 1, spec_l[0], hy_bias_l[0:1], dft_tabs, F32)
        hy_lat = _hy_long_conv(z1, 0, u_l, 2, spec_l[1], hy_bias_l[1:2], dft_tabs, BF16)[0]

        gates_l = _inproj_plain(x_lat, g_pre, m_lat[0], m_lat[1], w_gate, True)
        wbr = jnp.stack([w_br_da[l], w_br_ssd[l], w_br_mla[l], w_br_hy[l]]).astype(BF16)
        wout = w_out[l].astype(BF16)
        g_post = row(norm_mix_post[l])
        w1, w3, w2 = ffn_w1[l].astype(BF16), ffn_w3[l].astype(BF16), ffn_w2[l].astype(BF16)
        gf_pre, gf_post = row(norm_ffn_pre[l]), row(norm_ffn_post[l])

        x_lat = _merge(x_lat, (da_lat, ssd_lat, mla_lat, hy_lat), gates_l, wbr, wout, g_post, m_lat[2])
        x_lat = _ffn(x_lat, gf_pre, m_lat[3], m_lat[4], w1, w3, w2, gf_post, m_lat[5])

        if need_ctx:
            da_ctx = _flash(pc_['q_da'], pc_['k_da'], pc_['v_da'], None, None,
                            ncomp=2, heads=DA_HEADS, dq=LANES, da=da_fin)
            mla_ctx = _flash(pc_['q_m'], pc_['k_m'], pc_['v_m'], None, None, ncomp=1, heads=MLA_HEADS, dq=256)
            u_c = _dwconv(_inproj_plain(x_ctx, g_pre, m_ctx[0], m_ctx[1], w_hy, False), hw, hb, 3, False)
            g_c = _hyena_filter(n_ctx, lp).transpose(1, 0, 2).reshape(2 * n_ctx, HY_ORDER * HY_WIDTH)
            spec_c = _ctx_spectrum(g_c, ctx_tabs['fw_full'])
            hy_ctx = _hy_ctx(u_c, spec_c, ctx_tabs['fw'], ctx_tabs['iv'], hy_bias_l)
            gates_c = _inproj_plain(x_ctx, g_pre, m_ctx[0], m_ctx[1], w_gate, True)
            x_ctx = _merge(x_ctx, (da_ctx, ssd_ctx, mla_ctx, hy_ctx), gates_c, wbr, wout, g_post, m_ctx[2])
            x_ctx = _ffn(x_ctx, gf_pre, m_ctx[3], m_ctx[4], w1, w3, w2, gf_post, m_ctx[5])
    return x_lat
```

```python
import functools
import math

import jax
import jax.numpy as jnp
import numpy as np
from jax import lax
from jax.experimental import pallas as pl
from jax.experimental.pallas import tpu as pltpu

F32 = jnp.float32
BF16 = jnp.bfloat16
HIGHEST = lax.Precision.HIGHEST

D_MODEL = 1024
DEPTH = 2
GRID_W = 64
EPS = 1e-6
ROPE_BASE = 10000.0
ROT_DIM = 64

DA_HEADS = 4
DA_HEAD_DIM = 64
DA_V_DIM = 128
DA_QK_COLS = 512
DA_WIDTH = 512
COLS_DA = 1536

SSD_HEADS = 8
SSD_HEAD_DIM = 64
SSD_INNER = 512
SSD_GROUPS = 2
SSD_STATE = 128
SSD_CONV = 5
SSD_CHUNK = 128
SSD_XBC = 1024
COLS_SSD = 1552

MLA_HEADS = 4
MLA_Q_LORA = 384
MLA_KV_LORA = 256
MLA_NOPE = 128
MLA_ROPE = 64
MLA_V = 128
MLA_WIDTH = 512
COLS_MLA = 704

HY_WIDTH = 512
HY_ORDER = 2
HY_SHORT = 3
HY_EMB = 33
HY_TARGET = 1e-2
HY_DECAY_SHORT = 0.3
HY_DECAY_LONG = 1.5
COLS_HY = 1536

N_BRANCH = 4
COLS_GATE = 4096
FFN_HIDDEN = 2816

LANES = 128
SUBLANES = 8
VMEM_LIMIT_BYTES = 48 * 2 ** 20
NEG_BIG = -1e30
LOG2_E = 1.4426950408889634

ROW_TILE = 512
ATTN_TQ = 512
ATTN_TK_MAX = 2816
FFN_TH = 1408
DFT_B = 128


def _params(sem):
    return pltpu.CompilerParams(dimension_semantics=sem, vmem_limit_bytes=VMEM_LIMIT_BYTES)


def _full(shape):
    nd = len(shape)
    return pl.BlockSpec(shape, lambda *_: (0,) * nd)


def _dot(a, b):
    return jnp.dot(a, b, preferred_element_type=F32)


def _dot_nt(a, b):
    return lax.dot_general(a, b, (((1,), (1,)), ((), ())), preferred_element_type=F32)


def _dot_tn(a, b):
    return lax.dot_general(a, b, (((0,), (0,)), ((), ())), preferred_element_type=F32)


def _dot_hi(a, b):
    return jnp.dot(a, b, precision=HIGHEST, preferred_element_type=F32)


def _rms(x, g):
    return x * lax.rsqrt(jnp.mean(x * x, axis=-1, keepdims=True) + EPS) * g


def _norm_mod(x, g, shift, scale):
    return _rms(x, g) * (1.0 + scale) + shift


def _silu(x):
    return x * jax.nn.sigmoid(x)


def _rope128(x, cosf, sins, first_half):
    partner = jnp.where(first_half, pltpu.roll(x, 96, 1), pltpu.roll(x, 32, 1))
    return x * cosf + partner * sins


def _first_half_mask(rows):
    return (lax.broadcasted_iota(jnp.int32, (rows, LANES), 1) & 32) == 0


def _mod_kernel(c_ref, w_ref, b_ref, o_ref):
    act = _silu(c_ref[...]).astype(BF16)
    o_ref[...] = _dot(act, w_ref[...]) + b_ref[...]


def _modulation(c_rows, w, b):
    n = w.shape[1] // D_MODEL
    return pl.pallas_call(
        _mod_kernel,
        grid=(n,),
        in_specs=[_full((SUBLANES, D_MODEL)),
                  pl.BlockSpec((D_MODEL, D_MODEL), lambda j: (0, j)),
                  pl.BlockSpec((1, D_MODEL), lambda j: (0, j))],
        out_specs=pl.BlockSpec((SUBLANES, D_MODEL), lambda j: (0, j)),
        out_shape=jax.ShapeDtypeStruct((SUBLANES, w.shape[1]), F32),
        compiler_params=_params(("parallel",)),
        name="modulation",
    )(c_rows, w, b)


def _inproj_call(kernel, x, g, shift, scale, extra, extra_specs, out_dims, out_dtypes, name, extra_out=None):
    b, l, _ = x.shape
    tm = min(ROW_TILE, l)
    row = lambda d: pl.BlockSpec((1, tm, d), lambda bi, i: (bi, i, 0))
    vec = pl.BlockSpec((1, 1, D_MODEL), lambda bi, i: (bi, 0, 0))
    out_specs = [row(d) for d in out_dims]
    out_shape = [jax.ShapeDtypeStruct((b, l, d), dt) for d, dt in zip(out_dims, out_dtypes)]
    if extra_out is not None:
        out_specs.append(extra_out[0](tm))
        out_shape.append(extra_out[1])
    return pl.pallas_call(
        kernel,
        grid=(b, l // tm),
        in_specs=[row(D_MODEL), _full((1, D_MODEL)), vec, vec] + extra_specs(tm),
        out_specs=out_specs,
        out_shape=out_shape,
        compiler_params=_params(("parallel", "parallel")),
        name=name,
    )(x, g, shift, scale, *extra)


def _rope_specs(tm):
    return [pl.BlockSpec((tm, LANES), lambda bi, i: (i, 0))] * 2


def _inproj_da_kernel(*refs, rope):
    if rope:
        x_ref, g_ref, sh_ref, sc_ref, w_ref, wvt_ref, cos_ref, sin_ref, q_ref, k_ref, vt_ref = refs
    else:
        x_ref, g_ref, sh_ref, sc_ref, w_ref, wvt_ref, q_ref, k_ref, vt_ref = refs
    h = _norm_mod(x_ref[0], g_ref[...], sh_ref[0], sc_ref[0]).astype(BF16)
    tm = h.shape[0]
    first = _first_half_mask(tm)
    scale = DA_HEAD_DIM ** -0.5 * LOG2_E
    for out_ref, c0, is_q in ((q_ref, 0, True), (k_ref, 512, False)):
        res = _dot(h, w_ref[:, c0:c0 + 512])
        for i in range(4):
            t = res[:, i * LANES:(i + 1) * LANES]
            if rope:
                t = _rope128(t, cos_ref[...], sin_ref[...], first)
            if is_q:
                t = t * scale
            out_ref[0, :, i * LANES:(i + 1) * LANES] = t.astype(BF16)
    vt_ref[0] = _dot_nt(wvt_ref[...], h).astype(BF16)


def _tokens_last_spec(width, tm):
    return pl.BlockSpec((1, width, tm), lambda bi, i: (bi, 0, i))


def _inproj_da(x, g, shift, scale, w_qk, w_vt, rope_tabs):
    b, l, _ = x.shape
    rope = rope_tabs is not None
    extra = [w_qk, w_vt] + (list(rope_tabs) if rope else [])
    specs = lambda tm: [_full(w_qk.shape), _full(w_vt.shape)] + (_rope_specs(tm) if rope else [])
    return _inproj_call(functools.partial(_inproj_da_kernel, rope=rope), x, g, shift, scale, extra, specs,
                        (512, 512), (BF16, BF16), "inproj_da",
                        extra_out=(lambda tm: _tokens_last_spec(512, tm), jax.ShapeDtypeStruct((b, 512, l), BF16)))


def _inproj_mla_kernel(*refs, rope):
    if rope:
        (x_ref, g_ref, sh_ref, sc_ref, w_ref, gq_ref, wuq_ref, gkv_ref, wuk_ref, wuvt_ref, cos_ref, sin_ref,
         q_ref, k_ref, vt_ref) = refs
    else:
        (x_ref, g_ref, sh_ref, sc_ref, w_ref, gq_ref, wuq_ref, gkv_ref, wuk_ref, wuvt_ref,
         q_ref, k_ref, vt_ref) = refs
    h = _norm_mod(x_ref[0], g_ref[...], sh_ref[0], sc_ref[0]).astype(BF16)
    tm = h.shape[0]
    first = _first_half_mask(tm)
    res = _dot(h, w_ref[...])
    cq = _rms(res[:, :MLA_Q_LORA], gq_ref[...]).astype(BF16)
    ckv = _rms(res[:, MLA_Q_LORA:MLA_Q_LORA + MLA_KV_LORA], gkv_ref[...]).astype(BF16)
    kr = res[:, MLA_Q_LORA + MLA_KV_LORA:]
    q = _dot(cq, wuq_ref[...])
    kn = _dot(ckv, wuk_ref[...])
    if rope:
        kr = _rope128(kr, cos_ref[...], sin_ref[...], first)
    kr = kr.astype(BF16)
    scale = (MLA_NOPE + MLA_ROPE) ** -0.5 * LOG2_E
    for hh in range(MLA_HEADS):
        c0 = hh * 256
        qr = q[:, c0 + LANES:c0 + 256]
        if rope:
            qr = _rope128(qr, cos_ref[...], sin_ref[...], first)
        q_ref[0, :, c0:c0 + LANES] = (q[:, c0:c0 + LANES] * scale).astype(BF16)
        q_ref[0, :, c0 + LANES:c0 + 256] = (qr * scale).astype(BF16)
        k_ref[0, :, c0:c0 + LANES] = kn[:, hh * LANES:(hh + 1) * LANES].astype(BF16)
        k_ref[0, :, c0 + LANES:c0 + 256] = kr
    vt_ref[0] = _dot_nt(wuvt_ref[...], ckv).astype(BF16)


def _inproj_mla(x, g, shift, scale, w, gq, wuq, gkv, wuk, wuvt, rope_tabs):
    b, l, _ = x.shape
    rope = rope_tabs is not None
    consts = (w, gq, wuq, gkv, wuk, wuvt)
    extra = list(consts) + (list(rope_tabs) if rope else [])
    specs = lambda tm: [_full(a.shape) for a in consts] + (_rope_specs(tm) if rope else [])
    return _inproj_call(functools.partial(_inproj_mla_kernel, rope=rope), x, g, shift, scale, extra, specs,
                        (1024, 1024), (BF16, BF16), "inproj_mla",
                        extra_out=(lambda tm: _tokens_last_spec(512, tm), jax.ShapeDtypeStruct((b, 512, l), BF16)))


def _inproj_ssd_kernel(x_ref, g_ref, sh_ref, sc_ref, w_ref, z_ref, xbc_ref, dt_ref):
    h = _norm_mod(x_ref[0], g_ref[...], sh_ref[0], sc_ref[0]).astype(BF16)
    z_ref[0] = _dot(h, w_ref[:, 0:512])
    xbc_ref[0] = _dot(h, w_ref[:, 512:1536])
    dt_ref[0] = _dot(h, w_ref[:, 1536:1664])


def _inproj_ssd(x, g, shift, scale, w):
    return _inproj_call(_inproj_ssd_kernel, x, g, shift, scale, [w], lambda tm: [_full(w.shape)],
                        (512, 1024, LANES), (F32, F32, F32), "inproj_ssd")


def _inproj_plain_kernel(x_ref, g_ref, sh_ref, sc_ref, w_ref, o_ref, *, gate):
    h = _norm_mod(x_ref[0], g_ref[...], sh_ref[0], sc_ref[0]).astype(BF16)
    n = w_ref.shape[1]
    step = 512
    for c0 in range(0, n, step):
        res = _dot(h, w_ref[:, c0:c0 + step])
        if gate:
            res = jax.nn.sigmoid(res)
        o_ref[0, :, c0:c0 + step] = res.astype(o_ref.dtype)


def _inproj_plain(x, g, shift, scale, w, gate):
    return _inproj_call(functools.partial(_inproj_plain_kernel, gate=gate), x, g, shift, scale, [w],
                        lambda tm: [_full(w.shape)], (w.shape[1],), (BF16 if gate else F32,),
                        "inproj_gate" if gate else "inproj_hy")[0]


def _flash_kernel(*refs, ncomp, da_finish, lam_init):
    q_ref, k_ref, vt_ref = refs[:3]
    pos = 3
    if da_finish:
        lam_ref, sub_ref = refs[pos:pos + 2]
        pos += 2
    o_ref, m_ref, l_ref, acc_ref = refs[pos:pos + 4]
    j = pl.program_id(3)
    nk = pl.num_programs(3)
    q = q_ref[0]
    tq = q.shape[0]
    if ncomp == 2:
        lo = lax.broadcasted_iota(jnp.int32, (tq, LANES), 1) < DA_HEAD_DIM
        zero = jnp.zeros_like(q)
        qs = (jnp.where(lo, q, zero), jnp.where(lo, zero, q))
    else:
        qs = (q,)

    @pl.when(j == 0)
    def _():
        m_ref[...] = jnp.full(m_ref.shape, NEG_BIG, F32)
        l_ref[...] = jnp.zeros(l_ref.shape, F32)
        acc_ref[...] = jnp.zeros(acc_ref.shape, F32)

    k = k_ref[0, 0]
    vt = vt_ref[0]
    for c in range(ncomp):
        st = _dot_nt(k, qs[c])
        m_prev = m_ref[c]
        m_new = jnp.maximum(m_prev, jnp.max(st, axis=0, keepdims=True))
        alpha = jnp.exp2(m_prev - m_new)
        p = jnp.exp2(st - m_new)
        l_ref[c] = alpha * l_ref[c] + jnp.sum(p, axis=0, keepdims=True)
        acc_ref[c] = alpha * acc_ref[c] + _dot(vt, p.astype(BF16))
        m_ref[c] = m_new

    @pl.when(j == nk - 1)
    def _():
        o = acc_ref[0] / l_ref[0]
        if da_finish:
            o = o - lam_ref[...] * (acc_ref[1] / l_ref[1])
            ms = jnp.mean(o * o, axis=0, keepdims=True)
            o = o * lax.rsqrt(ms + EPS) * sub_ref[...] * (1.0 - lam_init)
        o_ref[0] = o.T.astype(o_ref.dtype)


def _key_tile(lk):
    best = LANES
    for t in range(LANES, min(lk, ATTN_TK_MAX) + 1, LANES):
        if lk % t == 0:
            best = t
    return best


def _flash(q, k, vt, *, ncomp, heads, dq, da=None):
    b, lq, _ = q.shape
    lk = k.shape[2]
    tq = min(ATTN_TQ, lq)
    nq = lq // tq
    tk = _key_tile(lk)
    nk = lk // tk
    in_specs = [pl.BlockSpec((1, tq, dq), lambda bi, h, i, j: (bi, i, h)),
                pl.BlockSpec((1, 1, tk, dq), lambda bi, h, i, j: (bi, h, j, 0)),
                pl.BlockSpec((1, LANES, tk), lambda bi, h, i, j: (bi, h, j))]
    args = [q, k, vt]
    lam_init = 0.0
    if da is not None:
        lam, subln, lam_init = da
        in_specs += [_full((1, 1)), _full((LANES, 1))]
        args += [lam, subln]
    kernel = functools.partial(_flash_kernel, ncomp=ncomp, da_finish=da is not None, lam_init=lam_init)
    return pl.pallas_call(
        kernel,
        grid=(b, heads, nq, nk),
        in_specs=in_specs,
        out_specs=pl.BlockSpec((1, tq, LANES), lambda bi, h, i, j: (bi, i, h)),
        out_shape=jax.ShapeDtypeStruct((b, lq, heads * LANES), BF16),
        scratch_shapes=[pltpu.VMEM((ncomp, 1, tq), F32), pltpu.VMEM((ncomp, 1, tq), F32),
                        pltpu.VMEM((ncomp, LANES, tq), F32)],
        compiler_params=_params(("parallel", "parallel", "parallel", "arbitrary")),
        name="flash_da" if ncomp == 2 else "flash_mla",
    )(*args)


def _dwconv_kernel(x_ref, w_ref, b_ref, o_ref, pad_ref, *, taps, act, chunk):
    l = x_ref.shape[1]
    pad_ref[0:SUBLANES, :] = jnp.zeros((SUBLANES, LANES), F32)
    pad_ref[SUBLANES + l:2 * SUBLANES + l, :] = jnp.zeros((SUBLANES, LANES), F32)
    pad_ref[SUBLANES:SUBLANES + l, :] = x_ref[0]
    w = w_ref[...]
    bias = b_ref[...]

    def body(i, carry):
        base = pl.multiple_of(i * chunk, chunk)
        acc = jnp.zeros((chunk, LANES), F32) + bias
        for t in range(taps):
            acc = acc + w[t:t + 1, :] * pad_ref[pl.ds(base + SUBLANES + t - taps // 2, chunk), :]
        if act:
            acc = _silu(acc)
        o_ref[0, 0, pl.ds(base, chunk), :] = acc
        return carry

    lax.fori_loop(0, l // chunk, body, 0)


def _dwconv(x, w, bias, groups, act):
    b, l, c = x.shape
    taps = w.shape[0]
    cg = c // groups
    per = cg // LANES
    chunk = min(512, l)
    return pl.pallas_call(
        functools.partial(_dwconv_kernel, taps=taps, act=act, chunk=chunk),
        grid=(b, c // LANES),
        in_specs=[pl.BlockSpec((1, l, LANES), lambda bi, ct: (bi, 0, ct)),
                  pl.BlockSpec((taps, LANES), lambda bi, ct: (0, ct)),
                  pl.BlockSpec((1, LANES), lambda bi, ct: (0, ct))],
        out_specs=pl.BlockSpec((1, 1, l, LANES), lambda bi, ct: (ct // per, bi, 0, ct % per)),
        out_shape=jax.ShapeDtypeStruct((groups, b, l, cg), F32),
        scratch_shapes=[pltpu.VMEM((l + 2 * SUBLANES, LANES), F32)],
        compiler_params=_params(("parallel", "parallel")),
        name="dwconv",
    )(x, w, bias)


def _ssd_kernel(*refs, reverse, finish):
    if finish:
        (xc_ref, dt_ref, init_ref, dtb_ref, a_ref, yo_ref, z_ref, d_ref, g_ref,
         y_ref, fin_ref, st_ref) = refs
    else:
        xc_ref, dt_ref, init_ref, dtb_ref, a_ref, y_ref, fin_ref, st_ref = refs
    j = pl.program_id(1)
    nc = pl.num_programs(1)
    q = SSD_CHUNK

    @pl.when(j == 0)
    def _():
        st_ref[...] = init_ref[0]

    xc = xc_ref[0, 0]
    pre = dt_ref[0] + dtb_ref[...]
    dt = jnp.maximum(pre, 0.0) + jnp.log1p(jnp.exp(-jnp.abs(pre)))
    a = dt * a_ref[...]
    r = lax.broadcasted_iota(jnp.int32, (q, q), 0)
    c = lax.broadcasted_iota(jnp.int32, (q, q), 1)
    tri = (c >= r) if reverse else (c <= r)
    cum = _dot_hi(tri.astype(F32), a)
    cum_t = cum.T
    tot = cum[0:1, :] if reverse else cum[q - 1:q, :]
    off = SSD_HEADS if reverse else 0
    lane_lo = c < SSD_HEAD_DIM
    row_lo = r < SSD_HEAD_DIM
    ys = []
    for g in range(SSD_GROUPS):
        bm = xc[:, 512 + g * 128:512 + (g + 1) * 128].astype(BF16)
        cm = xc[:, 768 + g * 128:768 + (g + 1) * 128].astype(BF16)
        cb = _dot_nt(cm, bm)
        for pp in range(2):
            hp = 2 * g + pp
            l0 = off + 2 * hp
            l1 = l0 + 1
            col0, col1 = cum[:, l0:l0 + 1], cum[:, l1:l1 + 1]
            dec0 = jnp.exp(jnp.where(tri, col0 - cum_t[l0:l0 + 1, :], NEG_BIG))
            dec1 = jnp.exp(jnp.where(tri, col1 - cum_t[l1:l1 + 1, :], NEG_BIG))
            xdt = xc[:, hp * 128:(hp + 1) * 128] * jnp.where(lane_lo, dt[:, l0:l0 + 1], dt[:, l1:l1 + 1])
            xdt_b = xdt.astype(BF16)
            zero = jnp.zeros_like(xdt_b)
            y = (_dot((cb * dec0).astype(BF16), jnp.where(lane_lo, xdt_b, zero))
                 + _dot((cb * dec1).astype(BF16), jnp.where(lane_lo, zero, xdt_b)))
            colp = jnp.where(lane_lo, col0, col1)
            totp = jnp.where(lane_lo, tot[:, l0:l0 + 1], tot[:, l1:l1 + 1])
            contrib = _dot_tn((xdt * jnp.exp(totp - colp)).astype(BF16), bm)
            prev = st_ref[hp]
            y = y + _dot_nt(cm, prev.astype(BF16)) * jnp.exp(colp)
            st_ref[hp] = prev * jnp.exp(jnp.where(row_lo, tot[:, l0:l0 + 1], tot[:, l1:l1 + 1])) + contrib
            if finish:
                ys.append(y)
            else:
                y_ref[0, :, hp * 128:(hp + 1) * 128] = y

    if finish:
        yt = jnp.concatenate(ys, axis=1) + yo_ref[0] + d_ref[...] * xc[:, :SSD_INNER]
        y_ref[0] = _rms(yt * _silu(z_ref[0]), g_ref[...]).astype(y_ref.dtype)

    @pl.when(j == nc - 1)
    def _():
        fin_ref[0] = st_ref[...]


def _ssd_scan(xc, dt_raw, init, dtb, a_row, reverse, fin=None):
    _, b, l, _ = xc.shape
    nc = l // SSD_CHUNK
    cidx = (lambda j: nc - 1 - j) if reverse else (lambda j: j)
    rows = lambda d: pl.BlockSpec((1, SSD_CHUNK, d), lambda bi, j: (bi, cidx(j), 0))
    st_spec = pl.BlockSpec((1, 4, 128, 128), lambda bi, j: (bi, 0, 0, 0))
    in_specs = [pl.BlockSpec((1, 1, SSD_CHUNK, SSD_XBC), lambda bi, j: (0, bi, cidx(j), 0)),
                rows(LANES), st_spec, _full((1, LANES)), _full((1, LANES))]
    args = [xc, dt_raw, init, dtb, a_row]
    finish = fin is not None
    if finish:
        y_other, z, d_row, g = fin
        in_specs += [rows(SSD_INNER), rows(SSD_INNER), _full((1, SSD_INNER)), _full((1, SSD_INNER))]
        args += [y_other, z, d_row, g]
    return pl.pallas_call(
        functools.partial(_ssd_kernel, reverse=reverse, finish=finish),
        grid=(b, nc),
        in_specs=in_specs,
        out_specs=[rows(SSD_INNER), st_spec],
        out_shape=[jax.ShapeDtypeStruct((b, l, SSD_INNER), BF16 if finish else F32),
                   jax.ShapeDtypeStruct((b, 4, 128, 128), F32)],
        scratch_shapes=[pltpu.VMEM((4, 128, 128), F32)],
        compiler_params=_params(("parallel", "arbitrary")),
        name="ssd_bwd" if reverse else "ssd_fwd",
    )(*args)


def _split_bf16(x):
    hi = x.astype(BF16)
    return hi, (x - hi.astype(F32)).astype(BF16)


def _dot_split(a3_ref, b):
    b_hi, b_lo = _split_bf16(b)
    return _dot(a3_ref[...], jnp.concatenate([b_hi, b_hi, b_lo], axis=0))


def _lane_tile(t, width):
    return jnp.concatenate([t] * (width // LANES), axis=1)


def _hy_dft_a_kernel(z_ref, fa_ref, tw_ref, o_ref, *, pair):
    a8, c = o_ref.shape[2], o_ref.shape[5]
    for i in range(SUBLANES):
        z = z_ref[0, 0, :, i, :]
        if pair:
            z = jnp.concatenate([z, z_ref[0, 1, :, i, :]], axis=0)
        res = _dot_split(fa_ref, z)
        half = res.shape[0] // 2
        xr, xi = res[:half], res[half:]
        tr, ti = _lane_tile(tw_ref[i, 0], c), _lane_tile(tw_ref[i, 1], c)
        o_ref[0, 0, :, i] = (xr * tr - xi * ti).reshape(a8, SUBLANES, c)
        o_ref[0, 1, :, i] = (xr * ti + xi * tr).reshape(a8, SUBLANES, c)


def _hy_dft_a(z5, group, fa3, tw, pair):
    _, s, rows, _, c = z5.shape
    dft_a = tw.shape[2]
    per = 2 if pair else 1
    p = s // per
    return pl.pallas_call(
        functools.partial(_hy_dft_a_kernel, pair=pair),
        grid=(p, DFT_B // SUBLANES),
        in_specs=[pl.BlockSpec((1, per, rows, SUBLANES, c), lambda pi, jb: (group, pi, 0, jb, 0)),
                  _full(fa3.shape),
                  pl.BlockSpec((SUBLANES, 2, dft_a, LANES), lambda pi, jb: (jb, 0, 0, 0))],
        out_specs=pl.BlockSpec((1, 2, dft_a // SUBLANES, SUBLANES, SUBLANES, c),
                               lambda pi, jb: (pi, 0, 0, jb, 0, 0)),
        out_shape=jax.ShapeDtypeStruct((p, 2, dft_a // SUBLANES, DFT_B, SUBLANES, c), F32),
        compiler_params=_params(("parallel", "parallel")),
        name="hyena_dft_a",
    )(z5, fa3, tw)


def _hy_dft_b_kernel(*refs, conv):
    if conv:
        y_ref, s_ref, mf_ref, mi_ref, o_ref = refs
    else:
        y_ref, mf_ref, o_ref = refs
    for kl in range(SUBLANES):
        y = jnp.concatenate([y_ref[0, 0, 0, :, kl, :], y_ref[0, 1, 0, :, kl, :]], axis=0)
        out = _dot_split(mf_ref, y)
        if conv:
            xr, xi = out[:DFT_B], out[DFT_B:]
            sr, si = s_ref[0, 0, :, kl, :], s_ref[1, 0, :, kl, :]
            out = _dot_split(mi_ref, jnp.concatenate([xr * sr - xi * si, xr * si + xi * sr], axis=0))
        o_ref[0, 0, 0, :, kl, :] = out[:DFT_B]
        o_ref[0, 1, 0, :, kl, :] = out[DFT_B:]


def _hy_dft_b(y1, mf3, spec_mi=None):
    p, _, a8, _, _, c = y1.shape
    conv = spec_mi is not None
    blk = pl.BlockSpec((1, 2, 1, DFT_B, SUBLANES, c), lambda kh, pi: (pi, 0, kh, 0, 0, 0))
    in_specs, args = [blk], [y1]
    if conv:
        in_specs += [pl.BlockSpec((2, 1, DFT_B, SUBLANES, c), lambda kh, pi: (0, kh, 0, 0, 0)), _full(mf3.shape),
                     _full(spec_mi[1].shape)]
        args += [spec_mi[0], mf3, spec_mi[1]]
    else:
        in_specs += [_full(mf3.shape)]
        args += [mf3]
    return pl.pallas_call(
        functools.partial(_hy_dft_b_kernel, conv=conv),
        grid=(a8, p),
        in_specs=in_specs,
        out_specs=blk,
        out_shape=jax.ShapeDtypeStruct(y1.shape, F32),
        compiler_params=_params(("parallel", "parallel")),
        name="hyena_dft_b" if conv else "hyena_spectrum_b",
    )(*args)


def _hy_dft_c_kernel(y_ref, tw_ref, fc_ref, gate_ref, zp_ref, bias_ref, o_ref):
    a8, c = y_ref.shape[2], y_ref.shape[5]
    na = o_ref.shape[1]
    bias = bias_ref[...]
    for i in range(SUBLANES):
        yr = y_ref[0, 0, :, i].reshape(a8 * SUBLANES, c)
        yi = y_ref[0, 1, :, i].reshape(a8 * SUBLANES, c)
        tr, ti = _lane_tile(tw_ref[i, 0], c), _lane_tile(tw_ref[i, 1], c)
        conv = _dot_split(fc_ref, jnp.concatenate([yr * tr + yi * ti, yi * tr - yr * ti], axis=0))
        for s in range(2):
            o_ref[s, :, i, :] = gate_ref[0, s, :, i, :] * (conv[s * na:(s + 1) * na] + bias * zp_ref[0, s, :, i, :])


def _hy_spectrum(g, tabs):
    o, n2, c = g.shape
    y1 = _hy_dft_a(g.reshape(1, o, n2 // DFT_B, DFT_B, c), 0, tabs['fa_full'], tabs['tw'], False)
    return _hy_dft_b(y1, tabs['mf'])


def _hy_long_conv(z5, zgroup, u5, gate_group, spec_l, bias_row, tabs):
    fc, tw = tabs['fc'], tabs['tw']
    _, b, na, _, c = z5.shape
    dft_a = 2 * na
    y1 = _hy_dft_a(z5, zgroup, tabs['fa'], tw, True)
    y2 = _hy_dft_b(y1, tabs['mf'], (spec_l, tabs['mi']))
    seq = lambda g: pl.BlockSpec((1, 2, na, SUBLANES, c), lambda pi, jb: (g, pi, 0, jb, 0))
    return pl.pallas_call(
        _hy_dft_c_kernel,
        grid=(b // 2, DFT_B // SUBLANES),
        in_specs=[pl.BlockSpec((1, 2, dft_a // SUBLANES, SUBLANES, SUBLANES, c), lambda pi, jb: (pi, 0, 0, jb, 0, 0)),
                  pl.BlockSpec((SUBLANES, 2, dft_a, LANES), lambda pi, jb: (jb, 0, 0, 0)),
                  _full(fc.shape), seq(gate_group), seq(zgroup), _full((1, c))],
        out_specs=pl.BlockSpec((2, na, SUBLANES, c), lambda pi, jb: (pi, 0, jb, 0)),
        out_shape=jax.ShapeDtypeStruct((b, na, DFT_B, c), F32),
        compiler_params=_params(("parallel", "parallel")),
        name="hyena_dft_c",
    )(y2, tw, fc, u5, z5, bias_row)


def _hy_ctx_kernel(v_ref, x1_ref, x2_ref, s_ref, fw_ref, iv_ref, bias_ref, o_ref):
    z = v_ref[0, 0]
    n2 = fw_ref.shape[0] // 2
    for o, gate_ref in enumerate((x1_ref, x2_ref)):
        x = _dot_split(fw_ref, z)
        xr, xi = x[:n2], x[n2:]
        sr, si = s_ref[0, :, o * HY_WIDTH:(o + 1) * HY_WIDTH], s_ref[1, :, o * HY_WIDTH:(o + 1) * HY_WIDTH]
        prod = jnp.concatenate([xr * sr - xi * si, xr * si + xi * sr], axis=0)
        conv = _dot_split(iv_ref, prod)
        z = gate_ref[0, 0] * (conv + bias_ref[o:o + 1, :] * z)
    o_ref[0] = z.astype(o_ref.dtype)


def _ctx_spectrum_kernel(g_ref, f_ref, o_ref):
    res = _dot_split(f_ref, g_ref[...])
    half = res.shape[0] // 2
    o_ref[0] = res[:half]
    o_ref[1] = res[half:]


def _ctx_spectrum(g, f3):
    n2, c = g.shape
    return pl.pallas_call(
        _ctx_spectrum_kernel,
        in_specs=[_full(g.shape), _full(f3.shape)],
        out_specs=_full((2, n2, c)),
        out_shape=jax.ShapeDtypeStruct((2, n2, c), F32),
        compiler_params=pltpu.CompilerParams(vmem_limit_bytes=VMEM_LIMIT_BYTES),
        name="hyena_ctx_spectrum",
    )(g, f3)


def _hy_ctx(u4, spec_c, fw, iv, bias):
    _, b, n, c = u4.shape
    grp = lambda g: pl.BlockSpec((1, 1, n, c), lambda bi: (g, bi, 0, 0))
    return pl.pallas_call(
        _hy_ctx_kernel,
        grid=(b,),
        in_specs=[grp(0), grp(1), grp(2), _full(spec_c.shape), _full(fw.shape), _full(iv.shape), _full(bias.shape)],
        out_specs=pl.BlockSpec((1, n, c), lambda bi: (bi, 0, 0)),
        out_shape=jax.ShapeDtypeStruct((b, n, c), BF16),
        compiler_params=_params(("parallel",)),
        name="hyena_ctx",
    )(u4, u4, u4, spec_c, fw, iv, bias)


def _merge_kernel(x_ref, da_ref, ssd_ref, mla_ref, hy_ref, gate_ref, wbr_ref, wout_ref, gpost_ref, mg_ref, o_ref):
    mixed = None
    for k, br_ref in enumerate((da_ref, ssd_ref, mla_ref, hy_ref)):
        t = gate_ref[0, :, k * D_MODEL:(k + 1) * D_MODEL].astype(F32) * _dot(br_ref[0].astype(BF16), wbr_ref[k])
        mixed = t if mixed is None else mixed + t
    y = _dot(mixed.astype(BF16), wout_ref[...])
    o_ref[0] = x_ref[0] + mg_ref[0] * _rms(y, gpost_ref[...])


def _merge(x, branches, gates, wbr, wout, gpost, mgate):
    b, l, _ = x.shape
    tm = min(ROW_TILE, l)
    row = lambda d: pl.BlockSpec((1, tm, d), lambda bi, i: (bi, i, 0))
    vec = pl.BlockSpec((1, 1, D_MODEL), lambda bi, i: (bi, 0, 0))
    return pl.pallas_call(
        _merge_kernel,
        grid=(b, l // tm),
        in_specs=[row(D_MODEL)] + [row(512)] * 4 + [row(COLS_GATE), _full(wbr.shape), _full(wout.shape),
                                                    _full((1, D_MODEL)), vec],
        out_specs=row(D_MODEL),
        out_shape=jax.ShapeDtypeStruct((b, l, D_MODEL), F32),
        compiler_params=_params(("parallel", "parallel")),
        name="merge",
    )(x, *branches, gates, wbr, wout, gpost, mgate)


def _ffn_kernel(x_ref, g_ref, sh_ref, sc_ref, w1_ref, w3_ref, w2_ref, gpost_ref, mg_ref, o_ref, h_ref, acc_ref):
    k = pl.program_id(2)

    @pl.when(k == 0)
    def _():
        h_ref[...] = _norm_mod(x_ref[0], g_ref[...], sh_ref[0], sc_ref[0]).astype(BF16)
        acc_ref[...] = jnp.zeros(acc_ref.shape, F32)

    h = h_ref[...]
    t = (_silu(_dot(h, w1_ref[...])) * _dot(h, w3_ref[...])).astype(BF16)
    acc_ref[...] += _dot(t, w2_ref[...])

    @pl.when(k == pl.num_programs(2) - 1)
    def _():
        o_ref[0] = x_ref[0] + mg_ref[0] * _rms(acc_ref[...], gpost_ref[...])


def _ffn(x, g, shift, scale, w1, w3, w2, gpost, mgate):
    b, l, _ = x.shape
    tm = min(ROW_TILE, l)
    nh = FFN_HIDDEN // FFN_TH
    row = pl.BlockSpec((1, tm, D_MODEL), lambda bi, i, k: (bi, i, 0))
    vec = pl.BlockSpec((1, 1, D_MODEL), lambda bi, i, k: (bi, 0, 0))
    return pl.pallas_call(
        _ffn_kernel,
        grid=(b, l // tm, nh),
        in_specs=[row, _full((1, D_MODEL)), vec, vec,
                  pl.BlockSpec((D_MODEL, FFN_TH), lambda bi, i, k: (0, k)),
                  pl.BlockSpec((D_MODEL, FFN_TH), lambda bi, i, k: (0, k)),
                  pl.BlockSpec((FFN_TH, D_MODEL), lambda bi, i, k: (k, 0)),
                  _full((1, D_MODEL)), vec],
        out_specs=row,
        out_shape=jax.ShapeDtypeStruct((b, l, D_MODEL), F32),
        scratch_shapes=[pltpu.VMEM((tm, D_MODEL), BF16), pltpu.VMEM((tm, D_MODEL), F32)],
        compiler_params=_params(("parallel", "parallel", "arbitrary")),
        name="ffn",
    )(x, g, shift, scale, w1, w3, w2, gpost, mgate)


def _rope_tables(n_lat):
    t = jnp.arange(n_lat)
    pos_row = (t // GRID_W).astype(F32)
    pos_col = (t % GRID_W).astype(F32)
    quarter = ROT_DIM // 4
    inv = ROPE_BASE ** (-jnp.arange(quarter, dtype=F32) / quarter)
    ang = jnp.concatenate([pos_row[:, None] * inv, pos_col[:, None] * inv], axis=-1)
    cos, sin = jnp.cos(ang), jnp.sin(ang)
    cosf = jnp.tile(cos, (1, 4))
    sins = jnp.tile(jnp.concatenate([-sin, sin], axis=-1), (1, 2))
    return cosf, sins


def _hyena_filter(n, lp):
    t = jnp.arange(n, dtype=F32)
    t_unit = t / (n - 1)
    bands = (HY_EMB - 1) // 2
    band_f = jnp.linspace(1e-4, bands - 1, bands, dtype=F32)
    w = 2.0 * math.pi * t / n
    feats = jnp.concatenate([t_unit[:, None], jnp.cos(w[:, None] * band_f), -jnp.sin(w[:, None] * band_f)], axis=-1)
    hid = jnp.sin(lp['hy_freq1'] * (feats @ lp['hy_w1'] + lp['hy_b1']))
    hid = jnp.sin(lp['hy_freq2'] * (hid @ lp['hy_w2'] + lp['hy_b2']))
    hid_rev = jnp.flip(hid, axis=0)
    w3 = lp['hy_w3'].reshape(-1, 2, HY_ORDER, HY_WIDTH)
    deltas = jnp.abs(jnp.linspace(math.log(HY_TARGET) / HY_DECAY_LONG, math.log(HY_TARGET) / HY_DECAY_SHORT,
                                  HY_WIDTH, dtype=F32))
    decay = jnp.exp(-t_unit[:, None] * deltas)
    decay_rev = jnp.exp(-jnp.flip(t_unit)[:, None] * deltas)
    orders = []
    for o in range(HY_ORDER):
        fwd = (hid @ w3[:, 0, o]) * decay
        bwd_rev = (hid_rev @ w3[:, 1, o]) * decay_rev
        orders.append(jnp.concatenate([fwd[:1] + bwd_rev[n - 1:], fwd[1:], jnp.zeros_like(fwd[:1]),
                                       bwd_rev[:n - 1]], axis=0))
    g = jnp.stack(orders)
    return g * lax.rsqrt(jnp.sum(g * g, axis=1, keepdims=True) + EPS)


def _stack3(m):
    m32 = jnp.asarray(np.ascontiguousarray(m), F32)
    hi = m32.astype(BF16)
    lo = (m32 - hi.astype(F32)).astype(BF16)
    return jnp.concatenate([hi, lo, hi], axis=1)


def _dft_tables(n):
    na = n // DFT_B
    dft_a = 2 * na
    ka = np.arange(dft_a)[:, None]
    a = np.arange(dft_a)[None, :]
    th = 2.0 * np.pi * (ka * a % dft_a) / dft_a
    ca, sa = np.cos(th), np.sin(th)
    kb = np.arange(DFT_B)[:, None]
    tb = 2.0 * np.pi * (kb * kb.T % DFT_B) / DFT_B
    cb, sb = np.cos(tb), np.sin(tb)
    fa_full = np.concatenate([ca, -sa], axis=0)
    bb = np.arange(DFT_B)[:, None]
    kk = np.arange(dft_a)[None, :]
    ph = 2.0 * np.pi * (bb * kk) / (2 * n)
    tw = np.stack([np.cos(ph), -np.sin(ph)], axis=1)
    tw = np.broadcast_to(tw[..., None], (DFT_B, 2, dft_a, LANES))
    can, san = ca[:, :na], sa[:, :na]
    return dict(
        fa=_stack3(np.block([[can, san], [-san, can]])),
        fa_full=_stack3(fa_full),
        mf=_stack3(np.block([[cb, sb], [-sb, cb]])),
        mi=_stack3(np.block([[cb, -sb], [sb, cb]])),
        fc=_stack3(np.block([[can.T, -san.T], [san.T, can.T]]) / (2 * n)),
        tw=jnp.asarray(np.ascontiguousarray(tw), F32))


def _ctx_dft_tables(n):
    k = np.arange(2 * n)[:, None]
    t = np.arange(2 * n)[None, :]
    th = 2.0 * np.pi * (k * t % (2 * n)) / (2 * n)
    fw_full = np.concatenate([np.cos(th), -np.sin(th)], axis=0)
    iv = np.concatenate([np.cos(th)[:n], -np.sin(th)[:n]], axis=1) / (2 * n)
    return dict(fw=_stack3(fw_full[:, :n]), fw_full=_stack3(fw_full), iv=_stack3(iv))


def kernel(x, c, ctx, c_ctx, mod_w, mod_b, norm_mix_pre, norm_mix_post, norm_ffn_pre, norm_ffn_post, w_in,
           da_lambda, da_subln, ssd_conv_w, ssd_conv_b, ssd_a_log, ssd_dt_bias, ssd_d, ssd_norm, mla_q_norm,
           mla_w_uq, mla_kv_norm, mla_w_ukv, hy_conv_w, hy_conv_b, hy_w1, hy_b1, hy_freq1, hy_w2, hy_b2, hy_freq2,
           hy_w3, hy_bias, w_br_da, w_br_ssd, w_br_mla, w_br_hy, w_out, ffn_w1, ffn_w3, ffn_w2):
    bsz, n_lat, _ = x.shape
    n_ctx = ctx.shape[1]
    depth = w_in.shape[0]
    rope_tabs = _rope_tables(n_lat)
    dft_tabs = _dft_tables(n_lat)
    ctx_tabs = _ctx_dft_tables(n_ctx)
    row = lambda v: v.reshape(1, -1).astype(F32)

    c_rows = jnp.concatenate([c, c_ctx[None, :], jnp.zeros((SUBLANES - bsz - 1, D_MODEL), F32)], axis=0)
    o_da, o_ssd, o_mla, o_hy = np.cumsum([COLS_DA, COLS_SSD, COLS_MLA, COLS_HY]).tolist()

    x_lat, x_ctx = x, ctx
    for l in range(depth):
        last = l == depth - 1
        lam_init = 0.8 - 0.6 * math.exp(-0.3 * l)
        mods = _modulation(c_rows, mod_w[l].astype(BF16), row(mod_b[l]))
        m_lat = [mods[:bsz, k * D_MODEL:(k + 1) * D_MODEL][:, None, :] for k in range(6)]
        m_ctx = [jnp.broadcast_to(mods[bsz, k * D_MODEL:(k + 1) * D_MODEL][None, None, :], (bsz, 1, D_MODEL))
                 for k in range(6)]

        wl = w_in[l]
        w_da_qk = wl[:, :2 * DA_QK_COLS].astype(BF16)
        w_da_vt = wl[:, 2 * DA_QK_COLS:o_da].T.astype(BF16)
        w_ssd = jnp.pad(wl[:, o_da:o_ssd], ((0, 0), (0, 1664 - COLS_SSD))).astype(BF16)
        w_mla = jnp.pad(wl[:, o_ssd:o_mla], ((0, 0), (0, 768 - COLS_MLA))).astype(BF16)
        w_hy = wl[:, o_mla:o_hy].astype(BF16)
        w_gate = wl[:, o_hy:].astype(BF16)
        wuq = mla_w_uq[l].reshape(MLA_Q_LORA, MLA_HEADS, MLA_NOPE + MLA_ROPE)
        wuq = jnp.pad(wuq, ((0, 0), (0, 0), (0, 256 - MLA_NOPE - MLA_ROPE))).reshape(MLA_Q_LORA, 1024).astype(BF16)
        wukv = mla_w_ukv[l].reshape(MLA_KV_LORA, MLA_HEADS, MLA_NOPE + MLA_V)
        wuk = wukv[:, :, :MLA_NOPE].reshape(MLA_KV_LORA, MLA_HEADS * MLA_NOPE).astype(BF16)
        wuvt = wukv[:, :, MLA_NOPE:].reshape(MLA_KV_LORA, MLA_WIDTH).T.astype(BF16)
        g_pre = row(norm_mix_pre[l])

        lv = da_lambda[l].astype(F32)
        lam = (jnp.exp(jnp.sum(lv[0] * lv[1])) - jnp.exp(jnp.sum(lv[2] * lv[3])) + lam_init).reshape(1, 1)
        da_fin = (lam, da_subln[l].reshape(-1, 1).astype(F32), lam_init)

        dtb = jnp.pad(ssd_dt_bias[l].reshape(1, -1), ((0, 0), (0, LANES - 2 * SSD_HEADS))).astype(F32)
        a_row = jnp.pad(-jnp.exp(ssd_a_log[l].astype(F32)).reshape(1, -1), ((0, 0), (0, LANES - 2 * SSD_HEADS)))
        d_row = row(jnp.repeat(ssd_d[l], SSD_HEAD_DIM))
        g_ssd = row(ssd_norm[l])

        lp = {'hy_w1': hy_w1[l], 'hy_b1': hy_b1[l], 'hy_freq1': hy_freq1[l], 'hy_w2': hy_w2[l],
              'hy_b2': hy_b2[l], 'hy_freq2': hy_freq2[l], 'hy_w3': hy_w3[l]}
        hy_bias_l = hy_bias[l].astype(F32)

        need_ctx = not last
        streams = [(x_lat, m_lat, rope_tabs), (x_ctx, m_ctx, None)]
        proj = []
        for xs, ms, rt in streams:
            q_da, k_da, v_da = _inproj_da(xs, g_pre, ms[0], ms[1], w_da_qk, w_da_vt, rt)
            z_s, xbc, dt_raw = _inproj_ssd(xs, g_pre, ms[0], ms[1], w_ssd)
            q_m, k_m, v_m = _inproj_mla(xs, g_pre, ms[0], ms[1], w_mla, row(mla_q_norm[l]), wuq,
                                        row(mla_kv_norm[l]), wuk, wuvt, rt)
            proj.append(dict(q_da=q_da, k_da=k_da, v_da=v_da, z=z_s, xbc=xbc, dt=dt_raw, q_m=q_m, k_m=k_m, v_m=v_m))
        pl_, pc_ = proj

        heads_first = lambda k, h: k.reshape(bsz, k.shape[1], h, -1).transpose(0, 2, 1, 3)
        kc_da, kc_m = heads_first(pc_['k_da'], DA_HEADS), heads_first(pc_['k_m'], MLA_HEADS)
        k_da = jnp.concatenate([heads_first(pl_['k_da'], DA_HEADS), kc_da], axis=2)
        k_m = jnp.concatenate([heads_first(pl_['k_m'], MLA_HEADS), kc_m], axis=2)
        vt_da = jnp.concatenate([pl_['v_da'], pc_['v_da']], axis=2)
        vt_m = jnp.concatenate([pl_['v_m'], pc_['v_m']], axis=2)
        da_lat = _flash(pl_['q_da'], k_da, vt_da, ncomp=2, heads=DA_HEADS, dq=LANES, da=da_fin)
        mla_lat = _flash(pl_['q_m'], k_m, vt_m, ncomp=1, heads=MLA_HEADS, dq=256)
        cw, cb_ = ssd_conv_w[l].astype(F32), row(ssd_conv_b[l])
        xc_c = _dwconv(pc_['xbc'], cw, cb_, 1, True)
        xc_l = _dwconv(pl_['xbc'], cw, cb_, 1, True)
        zeros = jnp.zeros((bsz, 4, 128, 128), F32)
        yf_c, sf = _ssd_scan(xc_c, pc_['dt'], zeros, dtb, a_row, False)
        yf_l, _ = _ssd_scan(xc_l, pl_['dt'], sf, dtb, a_row, False)
        ssd_ctx, sb = _ssd_scan(xc_c, pc_['dt'], zeros, dtb, a_row, True, fin=(yf_c, pc_['z'], d_row, g_ssd))
        ssd_lat, _ = _ssd_scan(xc_l, pl_['dt'], sb, dtb, a_row, True, fin=(yf_l, pl_['z'], d_row, g_ssd))
        hw, hb = hy_conv_w[l].astype(F32), row(hy_conv_b[l])
        u_l = _dwconv(_inproj_plain(x_lat, g_pre, m_lat[0], m_lat[1], w_hy, False), hw, hb, 3, False)
        spec_l = _hy_spectrum(_hyena_filter(n_lat, lp), dft_tabs)
        u5 = u_l.reshape(3, bsz, n_lat // DFT_B, DFT_B, HY_WIDTH)
        z1 = _hy_long_conv(u5, 0, u5, 1, spec_l[0], hy_bias_l[0:1], dft_tabs)
        hy_lat = _hy_long_conv(z1[None], 0, u5, 2, spec_l[1], hy_bias_l[1:2], dft_tabs).reshape(bsz, n_lat, HY_WIDTH)

        gates_l = _inproj_plain(x_lat, g_pre, m_lat[0], m_lat[1], w_gate, True)
        wbr = jnp.stack([w_br_da[l], w_br_ssd[l], w_br_mla[l], w_br_hy[l]]).astype(BF16)
        wout = w_out[l].astype(BF16)
        g_post = row(norm_mix_post[l])
        w1, w3, w2 = ffn_w1[l].astype(BF16), ffn_w3[l].astype(BF16), ffn_w2[l].astype(BF16)
        gf_pre, gf_post = row(norm_ffn_pre[l]), row(norm_ffn_post[l])

        x_lat = _merge(x_lat, (da_lat, ssd_lat, mla_lat, hy_lat), gates_l, wbr, wout, g_post, m_lat[2])
        x_lat = _ffn(x_lat, gf_pre, m_lat[3], m_lat[4], w1, w3, w2, gf_post, m_lat[5])

        if need_ctx:
            da_ctx = _flash(pc_['q_da'], kc_da, pc_['v_da'], ncomp=2, heads=DA_HEADS, dq=LANES, da=da_fin)
            mla_ctx = _flash(pc_['q_m'], kc_m, pc_['v_m'], ncomp=1, heads=MLA_HEADS, dq=256)
            u_c = _dwconv(_inproj_plain(x_ctx, g_pre, m_ctx[0], m_ctx[1], w_hy, False), hw, hb, 3, False)
            g_c = _hyena_filter(n_ctx, lp).transpose(1, 0, 2).reshape(2 * n_ctx, HY_ORDER * HY_WIDTH)
            spec_c = _ctx_spectrum(g_c, ctx_tabs['fw_full'])
            hy_ctx = _hy_ctx(u_c, spec_c, ctx_tabs['fw'], ctx_tabs['iv'], hy_bias_l)
            gates_c = _inproj_plain(x_ctx, g_pre, m_ctx[0], m_ctx[1], w_gate, True)
            x_ctx = _merge(x_ctx, (da_ctx, ssd_ctx, mla_ctx, hy_ctx), gates_c, wbr, wout, g_post, m_ctx[2])
            x_ctx = _ffn(x_ctx, gf_pre, m_ctx[3], m_ctx[4], w1, w3, w2, gf_post, m_ctx[5])
    return x_lat
```

```python
import functools
import math

import jax
import jax.numpy as jnp
import numpy as np
from jax import lax
from jax.experimental import pallas as pl
from jax.experimental.pallas import tpu as pltpu

F32 = jnp.float32
BF16 = jnp.bfloat16
HIGHEST = lax.Precision.HIGHEST

D_MODEL = 1024
DEPTH = 2
GRID_W = 64
EPS = 1e-6
ROPE_BASE = 10000.0
ROT_DIM = 64

DA_HEADS = 4
DA_HEAD_DIM = 64
DA_V_DIM = 128
DA_QK_COLS = 512
DA_WIDTH = 512
COLS_DA = 1536

SSD_HEADS = 8
SSD_HEAD_DIM = 64
SSD_INNER = 512
SSD_GROUPS = 2
SSD_STATE = 128
SSD_CONV = 5
SSD_CHUNK = 128
SSD_XBC = 1024
COLS_SSD = 1552

MLA_HEADS = 4
MLA_Q_LORA = 384
MLA_KV_LORA = 256
MLA_NOPE = 128
MLA_ROPE = 64
MLA_V = 128
MLA_WIDTH = 512
COLS_MLA = 704

HY_WIDTH = 512
HY_ORDER = 2
HY_SHORT = 3
HY_EMB = 33
HY_TARGET = 1e-2
HY_DECAY_SHORT = 0.3
HY_DECAY_LONG = 1.5
COLS_HY = 1536

N_BRANCH = 4
COLS_GATE = 4096
FFN_HIDDEN = 2816

LANES = 128
SUBLANES = 8
VMEM_LIMIT_BYTES = 48 * 2 ** 20
NEG_BIG = -1e30
LOG2_E = 1.4426950408889634

ROW_TILE = 512
ATTN_TQ = 1024
ATTN_TK_MAX = 2816
ATTN_PROBE = 256
ATTN_SUM_MAX = 2.0 ** 64
FFN_TH = 1408
DFT_B = 128


def _params(sem):
    return pltpu.CompilerParams(dimension_semantics=sem, vmem_limit_bytes=VMEM_LIMIT_BYTES)


def _full(shape):
    nd = len(shape)
    return pl.BlockSpec(shape, lambda *_: (0,) * nd)


def _dot(a, b):
    return jnp.dot(a, b, preferred_element_type=F32)


def _dot_nt(a, b):
    return lax.dot_general(a, b, (((1,), (1,)), ((), ())), preferred_element_type=F32)


def _dot_tn(a, b):
    return lax.dot_general(a, b, (((0,), (0,)), ((), ())), preferred_element_type=F32)


def _dot_hi(a, b):
    return jnp.dot(a, b, precision=HIGHEST, preferred_element_type=F32)


def _rms(x, g):
    return x * lax.rsqrt(jnp.mean(x * x, axis=-1, keepdims=True) + EPS) * g


def _norm_mod(x, g, shift, scale):
    return _rms(x, g) * (1.0 + scale) + shift


def _silu(x):
    return x * jax.nn.sigmoid(x)


def _rope128(x, cosf, sins, first_half):
    partner = jnp.where(first_half, pltpu.roll(x, 96, 1), pltpu.roll(x, 32, 1))
    return x * cosf + partner * sins


def _first_half_mask(rows):
    return (lax.broadcasted_iota(jnp.int32, (rows, LANES), 1) & 32) == 0


def _mod_kernel(c_ref, w_ref, b_ref, o_ref):
    act = _silu(c_ref[...]).astype(BF16)
    o_ref[...] = _dot(act, w_ref[...]) + b_ref[...]


def _modulation(c_rows, w, b):
    n = w.shape[1] // D_MODEL
    return pl.pallas_call(
        _mod_kernel,
        grid=(n,),
        in_specs=[_full((SUBLANES, D_MODEL)),
                  pl.BlockSpec((D_MODEL, D_MODEL), lambda j: (0, j)),
                  pl.BlockSpec((1, D_MODEL), lambda j: (0, j))],
        out_specs=pl.BlockSpec((SUBLANES, D_MODEL), lambda j: (0, j)),
        out_shape=jax.ShapeDtypeStruct((SUBLANES, w.shape[1]), F32),
        compiler_params=_params(("parallel",)),
        name="modulation",
    )(c_rows, w, b)


def _inproj_call(kernel, x, g, shift, scale, extra, extra_specs, out_dims, out_dtypes, name, extra_out=None):
    b, l, _ = x.shape
    tm = min(ROW_TILE, l)
    row = lambda d: pl.BlockSpec((1, tm, d), lambda bi, i: (bi, i, 0))
    vec = pl.BlockSpec((1, 1, D_MODEL), lambda bi, i: (bi, 0, 0))
    out_specs = [row(d) for d in out_dims]
    out_shape = [jax.ShapeDtypeStruct((b, l, d), dt) for d, dt in zip(out_dims, out_dtypes)]
    if extra_out is not None:
        out_specs.append(extra_out[0](tm))
        out_shape.append(extra_out[1])
    return pl.pallas_call(
        kernel,
        grid=(b, l // tm),
        in_specs=[row(D_MODEL), _full((1, D_MODEL)), vec, vec] + extra_specs(tm),
        out_specs=out_specs,
        out_shape=out_shape,
        compiler_params=_params(("parallel", "parallel")),
        name=name,
    )(x, g, shift, scale, *extra)


def _rope_specs(tm):
    return [pl.BlockSpec((tm, LANES), lambda bi, i: (i, 0))] * 2


def _inproj_da_kernel(*refs, rope):
    if rope:
        x_ref, g_ref, sh_ref, sc_ref, w_ref, wvt_ref, cos_ref, sin_ref, q_ref, k_ref, vt_ref = refs
    else:
        x_ref, g_ref, sh_ref, sc_ref, w_ref, wvt_ref, q_ref, k_ref, vt_ref = refs
    h = _norm_mod(x_ref[0], g_ref[...], sh_ref[0], sc_ref[0]).astype(BF16)
    tm = h.shape[0]
    first = _first_half_mask(tm)
    scale = DA_HEAD_DIM ** -0.5 * LOG2_E
    for out_ref, c0, is_q in ((q_ref, 0, True), (k_ref, 512, False)):
        res = _dot(h, w_ref[:, c0:c0 + 512])
        for i in range(4):
            t = res[:, i * LANES:(i + 1) * LANES]
            if rope:
                t = _rope128(t, cos_ref[...], sin_ref[...], first)
            if is_q:
                t = t * scale
            out_ref[0, :, i * LANES:(i + 1) * LANES] = t.astype(BF16)
    vt_ref[0] = _dot_nt(wvt_ref[...], h).astype(BF16)


def _tokens_last_spec(width, tm):
    return pl.BlockSpec((1, width, tm), lambda bi, i: (bi, 0, i))


def _inproj_da(x, g, shift, scale, w_qk, w_vt, rope_tabs):
    b, l, _ = x.shape
    rope = rope_tabs is not None
    extra = [w_qk, w_vt] + (list(rope_tabs) if rope else [])
    specs = lambda tm: [_full(w_qk.shape), _full(w_vt.shape)] + (_rope_specs(tm) if rope else [])
    return _inproj_call(functools.partial(_inproj_da_kernel, rope=rope), x, g, shift, scale, extra, specs,
                        (512, 512), (BF16, BF16), "inproj_da",
                        extra_out=(lambda tm: _tokens_last_spec(512, tm), jax.ShapeDtypeStruct((b, 512, l), BF16)))


def _inproj_mla_kernel(*refs, rope):
    if rope:
        (x_ref, g_ref, sh_ref, sc_ref, w_ref, gq_ref, wuq_ref, gkv_ref, wuk_ref, wuvt_ref, cos_ref, sin_ref,
         q_ref, k_ref, vt_ref) = refs
    else:
        (x_ref, g_ref, sh_ref, sc_ref, w_ref, gq_ref, wuq_ref, gkv_ref, wuk_ref, wuvt_ref,
         q_ref, k_ref, vt_ref) = refs
    h = _norm_mod(x_ref[0], g_ref[...], sh_ref[0], sc_ref[0]).astype(BF16)
    tm = h.shape[0]
    first = _first_half_mask(tm)
    res = _dot(h, w_ref[...])
    cq = _rms(res[:, :MLA_Q_LORA], gq_ref[...]).astype(BF16)
    ckv = _rms(res[:, MLA_Q_LORA:MLA_Q_LORA + MLA_KV_LORA], gkv_ref[...]).astype(BF16)
    kr = res[:, MLA_Q_LORA + MLA_KV_LORA:]
    q = _dot(cq, wuq_ref[...])
    kn = _dot(ckv, wuk_ref[...])
    if rope:
        kr = _rope128(kr, cos_ref[...], sin_ref[...], first)
    kr = kr.astype(BF16)
    scale = (MLA_NOPE + MLA_ROPE) ** -0.5 * LOG2_E
    for hh in range(MLA_HEADS):
        c0 = hh * 256
        qr = q[:, c0 + LANES:c0 + 256]
        if rope:
            qr = _rope128(qr, cos_ref[...], sin_ref[...], first)
        q_ref[0, :, c0:c0 + LANES] = (q[:, c0:c0 + LANES] * scale).astype(BF16)
        q_ref[0, :, c0 + LANES:c0 + 256] = (qr * scale).astype(BF16)
        k_ref[0, :, c0:c0 + LANES] = kn[:, hh * LANES:(hh + 1) * LANES].astype(BF16)
        k_ref[0, :, c0 + LANES:c0 + 256] = kr
    vt_ref[0] = _dot_nt(wuvt_ref[...], ckv).astype(BF16)


def _inproj_mla(x, g, shift, scale, w, gq, wuq, gkv, wuk, wuvt, rope_tabs):
    b, l, _ = x.shape
    rope = rope_tabs is not None
    consts = (w, gq, wuq, gkv, wuk, wuvt)
    extra = list(consts) + (list(rope_tabs) if rope else [])
    specs = lambda tm: [_full(a.shape) for a in consts] + (_rope_specs(tm) if rope else [])
    return _inproj_call(functools.partial(_inproj_mla_kernel, rope=rope), x, g, shift, scale, extra, specs,
                        (1024, 1024), (BF16, BF16), "inproj_mla",
                        extra_out=(lambda tm: _tokens_last_spec(512, tm), jax.ShapeDtypeStruct((b, 512, l), BF16)))


def _inproj_ssd_kernel(x_ref, g_ref, sh_ref, sc_ref, w_ref, z_ref, xbc_ref, dt_ref):
    h = _norm_mod(x_ref[0], g_ref[...], sh_ref[0], sc_ref[0]).astype(BF16)
    z_ref[0] = _dot(h, w_ref[:, 0:512])
    xbc_ref[0] = _dot(h, w_ref[:, 512:1536])
    dt_ref[0] = _dot(h, w_ref[:, 1536:1664])


def _inproj_ssd(x, g, shift, scale, w):
    return _inproj_call(_inproj_ssd_kernel, x, g, shift, scale, [w], lambda tm: [_full(w.shape)],
                        (512, 1024, LANES), (F32, F32, F32), "inproj_ssd")


def _inproj_hy_kernel(x_ref, g_ref, sh_ref, sc_ref, w_ref, o_ref):
    h = _norm_mod(x_ref[0], g_ref[...], sh_ref[0], sc_ref[0]).astype(BF16)
    step = 512
    for c0 in range(0, w_ref.shape[1], step):
        o_ref[0, :, c0:c0 + step] = _dot(h, w_ref[:, c0:c0 + step])


def _inproj_hy(x, g, shift, scale, w):
    return _inproj_call(_inproj_hy_kernel, x, g, shift, scale, [w], lambda tm: [_full(w.shape)],
                        (w.shape[1],), (F32,), "inproj_hy")[0]


def _flash_kernel(*refs, ncomp, da_finish, lam_init):
    q_ref, k_ref, vt_ref = refs[:3]
    pos = 3
    if da_finish:
        lam_ref, sub_ref = refs[pos:pos + 2]
        pos += 2
    o_ref, m_ref, l_ref, acc_ref = refs[pos:pos + 4]
    j = pl.program_id(3)
    nk = pl.num_programs(3)
    q = q_ref[0]
    tq = q.shape[0]
    if ncomp == 2:
        lo = lax.broadcasted_iota(jnp.int32, (tq, LANES), 1) < DA_HEAD_DIM
        zero = jnp.zeros_like(q)
        qs = (jnp.where(lo, q, zero), jnp.where(lo, zero, q))
    else:
        qs = (q,)

    @pl.when(j == 0)
    def _():
        for c in range(ncomp):
            m_ref[c] = jnp.max(_dot_nt(k_ref[0, 0, :ATTN_PROBE, :], qs[c]), axis=0, keepdims=True)
        l_ref[...] = jnp.zeros(l_ref.shape, F32)
        acc_ref[...] = jnp.zeros(acc_ref.shape, F32)

    k = k_ref[0, 0]
    vt = vt_ref[0]
    for c in range(ncomp):
        m = m_ref[c]
        p = jnp.exp2(_dot_nt(k, qs[c]) - m)
        l_add = jnp.sum(p, axis=0, keepdims=True)
        pv = _dot(vt, p.astype(BF16))
        safe = jnp.max(l_add) <= ATTN_SUM_MAX

        @pl.when(safe)
        def _():
            l_ref[c] += l_add
            acc_ref[c] += pv

        @pl.when(jnp.logical_not(safe))
        def _():
            st = _dot_nt(k, qs[c])
            m_new = jnp.maximum(m, jnp.max(st, axis=0, keepdims=True))
            alpha = jnp.exp2(m - m_new)
            p2 = jnp.exp2(st - m_new)
            l_ref[c] = alpha * l_ref[c] + jnp.sum(p2, axis=0, keepdims=True)
            acc_ref[c] = alpha * acc_ref[c] + _dot(vt, p2.astype(BF16))
            m_ref[c] = m_new

    @pl.when(j == nk - 1)
    def _():
        o = acc_ref[0] / l_ref[0]
        if da_finish:
            o = o - lam_ref[...] * (acc_ref[1] / l_ref[1])
            ms = jnp.mean(o * o, axis=0, keepdims=True)
            o = o * lax.rsqrt(ms + EPS) * sub_ref[...] * (1.0 - lam_init)
        o_ref[0] = o.T.astype(o_ref.dtype)


def _key_tile(lk):
    best = LANES
    for t in range(LANES, min(lk, ATTN_TK_MAX) + 1, LANES):
        if lk % t == 0:
            best = t
    return best


def _flash(q, k, vt, *, ncomp, heads, dq, da=None):
    b, lq, _ = q.shape
    lk = k.shape[2]
    tq = min(ATTN_TQ, lq)
    nq = lq // tq
    tk = _key_tile(lk)
    nk = lk // tk
    in_specs = [pl.BlockSpec((1, tq, dq), lambda bi, h, i, j: (bi, i, h)),
                pl.BlockSpec((1, 1, tk, dq), lambda bi, h, i, j: (bi, h, j, 0)),
                pl.BlockSpec((1, LANES, tk), lambda bi, h, i, j: (bi, h, j))]
    args = [q, k, vt]
    lam_init = 0.0
    if da is not None:
        lam, subln, lam_init = da
        in_specs += [_full((1, 1)), _full((LANES, 1))]
        args += [lam, subln]
    kernel = functools.partial(_flash_kernel, ncomp=ncomp, da_finish=da is not None, lam_init=lam_init)
    return pl.pallas_call(
        kernel,
        grid=(b, heads, nq, nk),
        in_specs=in_specs,
        out_specs=pl.BlockSpec((1, tq, LANES), lambda bi, h, i, j: (bi, i, h)),
        out_shape=jax.ShapeDtypeStruct((b, lq, heads * LANES), BF16),
        scratch_shapes=[pltpu.VMEM((ncomp, 1, tq), F32), pltpu.VMEM((ncomp, 1, tq), F32),
                        pltpu.VMEM((ncomp, LANES, tq), F32)],
        compiler_params=_params(("parallel", "parallel", "parallel", "arbitrary")),
        name="flash_da" if ncomp == 2 else "flash_mla",
    )(*args)


def _dwconv_kernel(x_ref, w_ref, b_ref, o_ref, pad_ref, *, taps, act, chunk):
    l = x_ref.shape[1]
    pad_ref[0:SUBLANES, :] = jnp.zeros((SUBLANES, LANES), F32)
    pad_ref[SUBLANES + l:2 * SUBLANES + l, :] = jnp.zeros((SUBLANES, LANES), F32)
    pad_ref[SUBLANES:SUBLANES + l, :] = x_ref[0]
    w = w_ref[...]
    bias = b_ref[...]

    def body(i, carry):
        base = pl.multiple_of(i * chunk, chunk)
        acc = jnp.zeros((chunk, LANES), F32) + bias
        for t in range(taps):
            acc = acc + w[t:t + 1, :] * pad_ref[pl.ds(base + SUBLANES + t - taps // 2, chunk), :]
        if act:
            acc = _silu(acc)
        o_ref[0, 0, pl.ds(base, chunk), :] = acc
        return carry

    lax.fori_loop(0, l // chunk, body, 0)


def _dwconv(x, w, bias, groups, act):
    b, l, c = x.shape
    taps = w.shape[0]
    cg = c // groups
    per = cg // LANES
    chunk = min(512, l)
    return pl.pallas_call(
        functools.partial(_dwconv_kernel, taps=taps, act=act, chunk=chunk),
        grid=(b, c // LANES),
        in_specs=[pl.BlockSpec((1, l, LANES), lambda bi, ct: (bi, 0, ct)),
                  pl.BlockSpec((taps, LANES), lambda bi, ct: (0, ct)),
                  pl.BlockSpec((1, LANES), lambda bi, ct: (0, ct))],
        out_specs=pl.BlockSpec((1, 1, l, LANES), lambda bi, ct: (ct // per, bi, 0, ct % per)),
        out_shape=jax.ShapeDtypeStruct((groups, b, l, cg), F32),
        scratch_shapes=[pltpu.VMEM((l + 2 * SUBLANES, LANES), F32)],
        compiler_params=_params(("parallel", "parallel")),
        name="dwconv",
    )(x, w, bias)


def _ssd_kernel(*refs, reverse, finish):
    if finish:
        (xc_ref, dt_ref, init_ref, dtb_ref, a_ref, yo_ref, z_ref, d_ref, g_ref,
         y_ref, fin_ref, st_ref) = refs
    else:
        xc_ref, dt_ref, init_ref, dtb_ref, a_ref, y_ref, fin_ref, st_ref = refs
    j = pl.program_id(1)
    nc = pl.num_programs(1)
    q = SSD_CHUNK

    @pl.when(j == 0)
    def _():
        st_ref[...] = init_ref[0]

    xc = xc_ref[0, 0]
    pre = dt_ref[0] + dtb_ref[...]
    dt = jnp.maximum(pre, 0.0) + jnp.log1p(jnp.exp(-jnp.abs(pre)))
    a = dt * a_ref[...]
    r = lax.broadcasted_iota(jnp.int32, (q, q), 0)
    c = lax.broadcasted_iota(jnp.int32, (q, q), 1)
    tri = (c >= r) if reverse else (c <= r)
    cum = _dot_hi(tri.astype(F32), a)
    cum_t = cum.T
    tot = cum[0:1, :] if reverse else cum[q - 1:q, :]
    off = SSD_HEADS if reverse else 0
    lane_lo = c < SSD_HEAD_DIM
    row_lo = r < SSD_HEAD_DIM
    ys = []
    for g in range(SSD_GROUPS):
        bm = xc[:, 512 + g * 128:512 + (g + 1) * 128].astype(BF16)
        cm = xc[:, 768 + g * 128:768 + (g + 1) * 128].astype(BF16)
        cb = _dot_nt(cm, bm)
        for pp in range(2):
            hp = 2 * g + pp
            l0 = off + 2 * hp
            l1 = l0 + 1
            col0, col1 = cum[:, l0:l0 + 1], cum[:, l1:l1 + 1]
            dec0 = jnp.exp(jnp.where(tri, col0 - cum_t[l0:l0 + 1, :], NEG_BIG))
            dec1 = jnp.exp(jnp.where(tri, col1 - cum_t[l1:l1 + 1, :], NEG_BIG))
            xdt = xc[:, hp * 128:(hp + 1) * 128] * jnp.where(lane_lo, dt[:, l0:l0 + 1], dt[:, l1:l1 + 1])
            xdt_b = xdt.astype(BF16)
            zero = jnp.zeros_like(xdt_b)
            y = (_dot((cb * dec0).astype(BF16), jnp.where(lane_lo, xdt_b, zero))
                 + _dot((cb * dec1).astype(BF16), jnp.where(lane_lo, zero, xdt_b)))
            colp = jnp.where(lane_lo, col0, col1)
            totp = jnp.where(lane_lo, tot[:, l0:l0 + 1], tot[:, l1:l1 + 1])
            contrib = _dot_tn((xdt * jnp.exp(totp - colp)).astype(BF16), bm)
            prev = st_ref[hp]
            y = y + _dot_nt(cm, prev.astype(BF16)) * jnp.exp(colp)
            st_ref[hp] = prev * jnp.exp(jnp.where(row_lo, tot[:, l0:l0 + 1], tot[:, l1:l1 + 1])) + contrib
            if finish:
                ys.append(y)
            else:
                y_ref[0, :, hp * 128:(hp + 1) * 128] = y

    if finish:
        yt = jnp.concatenate(ys, axis=1) + yo_ref[0] + d_ref[...] * xc[:, :SSD_INNER]
        y_ref[0] = _rms(yt * _silu(z_ref[0]), g_ref[...]).astype(y_ref.dtype)

    @pl.when(j == nc - 1)
    def _():
        fin_ref[0] = st_ref[...]


def _ssd_scan(xc, dt_raw, init, dtb, a_row, reverse, fin=None):
    _, b, l, _ = xc.shape
    nc = l // SSD_CHUNK
    cidx = (lambda j: nc - 1 - j) if reverse else (lambda j: j)
    rows = lambda d: pl.BlockSpec((1, SSD_CHUNK, d), lambda bi, j: (bi, cidx(j), 0))
    st_spec = pl.BlockSpec((1, 4, 128, 128), lambda bi, j: (bi, 0, 0, 0))
    in_specs = [pl.BlockSpec((1, 1, SSD_CHUNK, SSD_XBC), lambda bi, j: (0, bi, cidx(j), 0)),
                rows(LANES), st_spec, _full((1, LANES)), _full((1, LANES))]
    args = [xc, dt_raw, init, dtb, a_row]
    finish = fin is not None
    if finish:
        y_other, z, d_row, g = fin
        in_specs += [rows(SSD_INNER), rows(SSD_INNER), _full((1, SSD_INNER)), _full((1, SSD_INNER))]
        args += [y_other, z, d_row, g]
    return pl.pallas_call(
        functools.partial(_ssd_kernel, reverse=reverse, finish=finish),
        grid=(b, nc),
        in_specs=in_specs,
        out_specs=[rows(SSD_INNER), st_spec],
        out_shape=[jax.ShapeDtypeStruct((b, l, SSD_INNER), BF16 if finish else F32),
                   jax.ShapeDtypeStruct((b, 4, 128, 128), F32)],
        scratch_shapes=[pltpu.VMEM((4, 128, 128), F32)],
        compiler_params=_params(("parallel", "arbitrary")),
        name="ssd_bwd" if reverse else "ssd_fwd",
    )(*args)


def _split_bf16(x):
    hi = x.astype(BF16)
    return hi, (x - hi.astype(F32)).astype(BF16)


def _dot_split(a3_ref, b):
    b_hi, b_lo = _split_bf16(b)
    return _dot(a3_ref[...], jnp.concatenate([b_hi, b_hi, b_lo], axis=0))


def _lane_tile(t, width):
    return jnp.concatenate([t] * (width // LANES), axis=1)


def _hy_dft_a_kernel(z_ref, fa_ref, tw_ref, o_ref, *, pair):
    a8, c = o_ref.shape[2], o_ref.shape[5]
    for i in range(SUBLANES):
        if pair:
            z = jnp.concatenate([z_ref[0, 0, :, i, :], z_ref[0, 1, :, i, :]], axis=0)
        else:
            z = z_ref[0, 0, i]
        res = _dot_split(fa_ref, z)
        half = res.shape[0] // 2
        xr, xi = res[:half], res[half:]
        tr, ti = _lane_tile(tw_ref[i, 0], c), _lane_tile(tw_ref[i, 1], c)
        o_ref[0, 0, :, i] = (xr * tr - xi * ti).reshape(a8, SUBLANES, c)
        o_ref[0, 1, :, i] = (xr * ti + xi * tr).reshape(a8, SUBLANES, c)


def _hy_dft_a(z5, group, fa3, tw, pair):
    s, c = z5.shape[1], z5.shape[4]
    dft_a = tw.shape[2]
    if pair:
        p = s // 2
        z_spec = pl.BlockSpec((1, 2, z5.shape[2], SUBLANES, c), lambda pi, jb: (group, pi, 0, jb, 0))
    else:
        p = s
        z_spec = pl.BlockSpec((1, 1, SUBLANES, z5.shape[3], c), lambda pi, jb: (group, pi, jb, 0, 0))
    return pl.pallas_call(
        functools.partial(_hy_dft_a_kernel, pair=pair),
        grid=(p, DFT_B // SUBLANES),
        in_specs=[z_spec,
                  _full(fa3.shape),
                  pl.BlockSpec((SUBLANES, 2, dft_a, LANES), lambda pi, jb: (jb, 0, 0, 0))],
        out_specs=pl.BlockSpec((1, 2, dft_a // SUBLANES, SUBLANES, SUBLANES, c),
                               lambda pi, jb: (pi, 0, 0, jb, 0, 0)),
        out_shape=jax.ShapeDtypeStruct((p, 2, dft_a // SUBLANES, DFT_B, SUBLANES, c), F32),
        compiler_params=_params(("parallel", "parallel")),
        name="hyena_dft_a",
    )(z5, fa3, tw)


def _hy_dft_b_kernel(*refs, conv):
    if conv:
        y_ref, s_ref, mf_ref, mi_ref, o_ref = refs
    else:
        y_ref, mf_ref, o_ref = refs
    for kl in range(SUBLANES):
        y = jnp.concatenate([y_ref[0, 0, 0, :, kl, :], y_ref[0, 1, 0, :, kl, :]], axis=0)
        out = _dot_split(mf_ref, y)
        if conv:
            xr, xi = out[:DFT_B], out[DFT_B:]
            sr, si = s_ref[0, 0, :, kl, :], s_ref[1, 0, :, kl, :]
            out = _dot_split(mi_ref, jnp.concatenate([xr * sr - xi * si, xr * si + xi * sr], axis=0))
        o_ref[0, 0, 0, :, kl, :] = out[:DFT_B]
        o_ref[0, 1, 0, :, kl, :] = out[DFT_B:]


def _hy_dft_b(y1, mf3, spec_mi=None):
    p, _, a8, _, _, c = y1.shape
    conv = spec_mi is not None
    blk = pl.BlockSpec((1, 2, 1, DFT_B, SUBLANES, c), lambda kh, pi: (pi, 0, kh, 0, 0, 0))
    in_specs, args = [blk], [y1]
    if conv:
        in_specs += [pl.BlockSpec((2, 1, DFT_B, SUBLANES, c), lambda kh, pi: (0, kh, 0, 0, 0)), _full(mf3.shape),
                     _full(spec_mi[1].shape)]
        args += [spec_mi[0], mf3, spec_mi[1]]
    else:
        in_specs += [_full(mf3.shape)]
        args += [mf3]
    return pl.pallas_call(
        functools.partial(_hy_dft_b_kernel, conv=conv),
        grid=(a8, p),
        in_specs=in_specs,
        out_specs=blk,
        out_shape=jax.ShapeDtypeStruct(y1.shape, F32),
        compiler_params=_params(("parallel", "parallel")),
        name="hyena_dft_b" if conv else "hyena_spectrum_b",
    )(*args)


def _hy_dft_c_kernel(y_ref, tw_ref, fc_ref, gate_ref, zp_ref, bias_ref, o_ref):
    a8, c = y_ref.shape[2], y_ref.shape[5]
    na = o_ref.shape[1]
    bias = bias_ref[...]
    for i in range(SUBLANES):
        yr = y_ref[0, 0, :, i].reshape(a8 * SUBLANES, c)
        yi = y_ref[0, 1, :, i].reshape(a8 * SUBLANES, c)
        tr, ti = _lane_tile(tw_ref[i, 0], c), _lane_tile(tw_ref[i, 1], c)
        conv = _dot_split(fc_ref, jnp.concatenate([yr * tr + yi * ti, yi * tr - yr * ti], axis=0))
        for s in range(2):
            o_ref[s, :, i, :] = gate_ref[0, s, :, i, :] * (conv[s * na:(s + 1) * na] + bias * zp_ref[0, s, :, i, :])


def _hy_spectrum(g, tabs):
    y1 = _hy_dft_a(g[None], 0, tabs['fa_full'], tabs['tw'], False)
    return _hy_dft_b(y1, tabs['mf'])


def _hy_long_conv(z5, zgroup, u5, gate_group, spec_l, bias_row, tabs):
    fc, tw = tabs['fc'], tabs['tw']
    _, b, na, _, c = z5.shape
    dft_a = 2 * na
    y1 = _hy_dft_a(z5, zgroup, tabs['fa'], tw, True)
    y2 = _hy_dft_b(y1, tabs['mf'], (spec_l, tabs['mi']))
    seq = lambda g: pl.BlockSpec((1, 2, na, SUBLANES, c), lambda pi, jb: (g, pi, 0, jb, 0))
    return pl.pallas_call(
        _hy_dft_c_kernel,
        grid=(b // 2, DFT_B // SUBLANES),
        in_specs=[pl.BlockSpec((1, 2, dft_a // SUBLANES, SUBLANES, SUBLANES, c), lambda pi, jb: (pi, 0, 0, jb, 0, 0)),
                  pl.BlockSpec((SUBLANES, 2, dft_a, LANES), lambda pi, jb: (jb, 0, 0, 0)),
                  _full(fc.shape), seq(gate_group), seq(zgroup), _full((1, c))],
        out_specs=pl.BlockSpec((2, na, SUBLANES, c), lambda pi, jb: (pi, 0, jb, 0)),
        out_shape=jax.ShapeDtypeStruct((b, na, DFT_B, c), F32),
        compiler_params=_params(("parallel", "parallel")),
        name="hyena_dft_c",
    )(y2, tw, fc, u5, z5, bias_row)


def _hy_ctx_kernel(v_ref, x1_ref, x2_ref, s_ref, fw_ref, iv_ref, bias_ref, o_ref):
    z = v_ref[0, 0]
    n2 = fw_ref.shape[0] // 2
    for o, gate_ref in enumerate((x1_ref, x2_ref)):
        x = _dot_split(fw_ref, z)
        xr, xi = x[:n2], x[n2:]
        sr, si = s_ref[0, :, o * HY_WIDTH:(o + 1) * HY_WIDTH], s_ref[1, :, o * HY_WIDTH:(o + 1) * HY_WIDTH]
        prod = jnp.concatenate([xr * sr - xi * si, xr * si + xi * sr], axis=0)
        conv = _dot_split(iv_ref, prod)
        z = gate_ref[0, 0] * (conv + bias_ref[o:o + 1, :] * z)
    o_ref[0] = z.astype(o_ref.dtype)


def _ctx_spectrum_kernel(g_ref, f_ref, o_ref):
    res = _dot_split(f_ref, g_ref[...])
    half = res.shape[0] // 2
    o_ref[0] = res[:half]
    o_ref[1] = res[half:]


def _ctx_spectrum(g, f3):
    n2, c = g.shape
    return pl.pallas_call(
        _ctx_spectrum_kernel,
        in_specs=[_full(g.shape), _full(f3.shape)],
        out_specs=_full((2, n2, c)),
        out_shape=jax.ShapeDtypeStruct((2, n2, c), F32),
        compiler_params=pltpu.CompilerParams(vmem_limit_bytes=VMEM_LIMIT_BYTES),
        name="hyena_ctx_spectrum",
    )(g, f3)


def _hy_ctx(u4, spec_c, fw, iv, bias):
    _, b, n, c = u4.shape
    grp = lambda g: pl.BlockSpec((1, 1, n, c), lambda bi: (g, bi, 0, 0))
    return pl.pallas_call(
        _hy_ctx_kernel,
        grid=(b,),
        in_specs=[grp(0), grp(1), grp(2), _full(spec_c.shape), _full(fw.shape), _full(iv.shape), _full(bias.shape)],
        out_specs=pl.BlockSpec((1, n, c), lambda bi: (bi, 0, 0)),
        out_shape=jax.ShapeDtypeStruct((b, n, c), BF16),
        compiler_params=_params(("parallel",)),
        name="hyena_ctx",
    )(u4, u4, u4, spec_c, fw, iv, bias)


def _merge_kernel(x_ref, g_ref, sh_ref, sc_ref, da_ref, ssd_ref, mla_ref, hy_ref, wgate_ref, wbr_ref, wout_ref,
                  gpost_ref, mg_ref, o_ref):
    x = x_ref[0]
    h = _norm_mod(x, g_ref[...], sh_ref[0], sc_ref[0]).astype(BF16)
    mixed = None
    for k, br_ref in enumerate((da_ref, ssd_ref, mla_ref, hy_ref)):
        gate = jax.nn.sigmoid(_dot(h, wgate_ref[:, k * D_MODEL:(k + 1) * D_MODEL]))
        t = gate * _dot(br_ref[0].astype(BF16), wbr_ref[k])
        mixed = t if mixed is None else mixed + t
    y = _dot(mixed.astype(BF16), wout_ref[...])
    o_ref[0] = x + mg_ref[0] * _rms(y, gpost_ref[...])


def _merge(x, g, shift, scale, branches, wgate, wbr, wout, gpost, mgate):
    b, l, _ = x.shape
    tm = min(ROW_TILE, l)
    row = lambda d: pl.BlockSpec((1, tm, d), lambda bi, i: (bi, i, 0))
    vec = pl.BlockSpec((1, 1, D_MODEL), lambda bi, i: (bi, 0, 0))
    return pl.pallas_call(
        _merge_kernel,
        grid=(b, l // tm),
        in_specs=[row(D_MODEL), _full((1, D_MODEL)), vec, vec] + [row(512)] * 4
                 + [_full(wgate.shape), _full(wbr.shape), _full(wout.shape), _full((1, D_MODEL)), vec],
        out_specs=row(D_MODEL),
        out_shape=jax.ShapeDtypeStruct((b, l, D_MODEL), F32),
        compiler_params=_params(("parallel", "parallel")),
        name="merge",
    )(x, g, shift, scale, *branches, wgate, wbr, wout, gpost, mgate)


def _ffn_kernel(x_ref, g_ref, sh_ref, sc_ref, w1_ref, w3_ref, w2_ref, gpost_ref, mg_ref, o_ref, h_ref, acc_ref):
    k = pl.program_id(2)

    @pl.when(k == 0)
    def _():
        h_ref[...] = _norm_mod(x_ref[0], g_ref[...], sh_ref[0], sc_ref[0]).astype(BF16)
        acc_ref[...] = jnp.zeros(acc_ref.shape, F32)

    h = h_ref[...]
    t = (_silu(_dot(h, w1_ref[...])) * _dot(h, w3_ref[...])).astype(BF16)
    acc_ref[...] += _dot(t, w2_ref[...])

    @pl.when(k == pl.num_programs(2) - 1)
    def _():
        o_ref[0] = x_ref[0] + mg_ref[0] * _rms(acc_ref[...], gpost_ref[...])


def _ffn(x, g, shift, scale, w1, w3, w2, gpost, mgate):
    b, l, _ = x.shape
    tm = min(ROW_TILE, l)
    nh = FFN_HIDDEN // FFN_TH
    row = pl.BlockSpec((1, tm, D_MODEL), lambda bi, i, k: (bi, i, 0))
    vec = pl.BlockSpec((1, 1, D_MODEL), lambda bi, i, k: (bi, 0, 0))
    return pl.pallas_call(
        _ffn_kernel,
        grid=(b, l // tm, nh),
        in_specs=[row, _full((1, D_MODEL)), vec, vec,
                  pl.BlockSpec((D_MODEL, FFN_TH), lambda bi, i, k: (0, k)),
                  pl.BlockSpec((D_MODEL, FFN_TH), lambda bi, i, k: (0, k)),
                  pl.BlockSpec((FFN_TH, D_MODEL), lambda bi, i, k: (k, 0)),
                  _full((1, D_MODEL)), vec],
        out_specs=row,
        out_shape=jax.ShapeDtypeStruct((b, l, D_MODEL), F32),
        scratch_shapes=[pltpu.VMEM((tm, D_MODEL), BF16), pltpu.VMEM((tm, D_MODEL), F32)],
        compiler_params=_params(("parallel", "parallel", "arbitrary")),
        name="ffn",
    )(x, g, shift, scale, w1, w3, w2, gpost, mgate)


def _rope_tables(n_lat):
    t = jnp.arange(n_lat)
    pos_row = (t // GRID_W).astype(F32)
    pos_col = (t % GRID_W).astype(F32)
    quarter = ROT_DIM // 4
    inv = ROPE_BASE ** (-jnp.arange(quarter, dtype=F32) / quarter)
    ang = jnp.concatenate([pos_row[:, None] * inv, pos_col[:, None] * inv], axis=-1)
    cos, sin = jnp.cos(ang), jnp.sin(ang)
    cosf = jnp.tile(cos, (1, 4))
    sins = jnp.tile(jnp.concatenate([-sin, sin], axis=-1), (1, 2))
    return cosf, sins


def _hyena_filter(n, lp):
    na = n // DFT_B
    t = jnp.arange(n, dtype=F32)
    t_unit = t / (n - 1)
    bands = (HY_EMB - 1) // 2
    band_f = jnp.linspace(1e-4, bands - 1, bands, dtype=F32)
    w = 2.0 * math.pi * t / n
    feats = jnp.concatenate([t_unit[:, None], jnp.cos(w[:, None] * band_f), -jnp.sin(w[:, None] * band_f)], axis=-1)
    hid = jnp.sin(lp['hy_freq1'] * (feats @ lp['hy_w1'] + lp['hy_b1']))
    hid = jnp.sin(lp['hy_freq2'] * (hid @ lp['hy_w2'] + lp['hy_b2']))
    w3 = lp['hy_w3'].reshape(-1, 2, HY_ORDER, HY_WIDTH)
    deltas = jnp.abs(jnp.linspace(math.log(HY_TARGET) / HY_DECAY_LONG, math.log(HY_TARGET) / HY_DECAY_SHORT,
                                  HY_WIDTH, dtype=F32))
    lag_f = DFT_B * np.arange(na)[None, :] + np.arange(DFT_B)[:, None]
    lag_b = n - lag_f
    zero_tap = jnp.asarray(lag_b == n)[:, :, None]
    lag_b = np.minimum(lag_b, n - 1)

    def taps(lags, w_dir):
        flat = lags.reshape(-1)
        vals = (hid[flat] @ w_dir) * jnp.exp(-t_unit[flat][:, None] * deltas)
        return vals.reshape(DFT_B, na, HY_WIDTH)

    orders = []
    for o in range(HY_ORDER):
        fwd = taps(lag_f, w3[:, 0, o])
        fwd = fwd.at[0, 0].add(hid[0] @ w3[:, 1, o])
        bwd = jnp.where(zero_tap, 0.0, taps(lag_b, w3[:, 1, o]))
        orders.append(jnp.concatenate([fwd, bwd], axis=1))
    g = jnp.stack(orders)
    return g * lax.rsqrt(jnp.sum(g * g, axis=(1, 2), keepdims=True) + EPS)


def _stack3(m):
    m32 = jnp.asarray(np.ascontiguousarray(m), F32)
    hi = m32.astype(BF16)
    lo = (m32 - hi.astype(F32)).astype(BF16)
    return jnp.concatenate([hi, lo, hi], axis=1)


def _dft_tables(n):
    na = n // DFT_B
    dft_a = 2 * na
    ka = np.arange(dft_a)[:, None]
    a = np.arange(dft_a)[None, :]
    th = 2.0 * np.pi * (ka * a % dft_a) / dft_a
    ca, sa = np.cos(th), np.sin(th)
    kb = np.arange(DFT_B)[:, None]
    tb = 2.0 * np.pi * (kb * kb.T % DFT_B) / DFT_B
    cb, sb = np.cos(tb), np.sin(tb)
    fa_full = np.concatenate([ca, -sa], axis=0)
    bb = np.arange(DFT_B)[:, None]
    kk = np.arange(dft_a)[None, :]
    ph = 2.0 * np.pi * (bb * kk) / (2 * n)
    tw = np.stack([np.cos(ph), -np.sin(ph)], axis=1)
    tw = np.broadcast_to(tw[..., None], (DFT_B, 2, dft_a, LANES))
    can, san = ca[:, :na], sa[:, :na]
    return dict(
        fa=_stack3(np.block([[can, san], [-san, can]])),
        fa_full=_stack3(fa_full),
        mf=_stack3(np.block([[cb, sb], [-sb, cb]])),
        mi=_stack3(np.block([[cb, -sb], [sb, cb]])),
        fc=_stack3(np.block([[can.T, -san.T], [san.T, can.T]]) / (2 * n)),
        tw=jnp.asarray(np.ascontiguousarray(tw), F32))


def _ctx_dft_tables(n):
    k = np.arange(2 * n)[:, None]
    t = np.arange(2 * n)[None, :]
    th = 2.0 * np.pi * (k * t % (2 * n)) / (2 * n)
    fw_full = np.concatenate([np.cos(th), -np.sin(th)], axis=0)
    iv = np.concatenate([np.cos(th)[:n], -np.sin(th)[:n]], axis=1) / (2 * n)
    return dict(fw=_stack3(fw_full[:, :n]), fw_full=_stack3(fw_full), iv=_stack3(iv))


def kernel(x, c, ctx, c_ctx, mod_w, mod_b, norm_mix_pre, norm_mix_post, norm_ffn_pre, norm_ffn_post, w_in,
           da_lambda, da_subln, ssd_conv_w, ssd_conv_b, ssd_a_log, ssd_dt_bias, ssd_d, ssd_norm, mla_q_norm,
           mla_w_uq, mla_kv_norm, mla_w_ukv, hy_conv_w, hy_conv_b, hy_w1, hy_b1, hy_freq1, hy_w2, hy_b2, hy_freq2,
           hy_w3, hy_bias, w_br_da, w_br_ssd, w_br_mla, w_br_hy, w_out, ffn_w1, ffn_w3, ffn_w2):
    bsz, n_lat, _ = x.shape
    n_ctx = ctx.shape[1]
    depth = w_in.shape[0]
    rope_tabs = _rope_tables(n_lat)
    dft_tabs = _dft_tables(n_lat)
    ctx_tabs = _ctx_dft_tables(n_ctx)
    row = lambda v: v.reshape(1, -1).astype(F32)

    c_rows = jnp.concatenate([c, c_ctx[None, :], jnp.zeros((SUBLANES - bsz - 1, D_MODEL), F32)], axis=0)
    o_da, o_ssd, o_mla, o_hy = np.cumsum([COLS_DA, COLS_SSD, COLS_MLA, COLS_HY]).tolist()

    x_lat, x_ctx = x, ctx
    for l in range(depth):
        last = l == depth - 1
        lam_init = 0.8 - 0.6 * math.exp(-0.3 * l)
        mods = _modulation(c_rows, mod_w[l].astype(BF16), row(mod_b[l]))
        m_lat = [mods[:bsz, k * D_MODEL:(k + 1) * D_MODEL][:, None, :] for k in range(6)]
        m_ctx = [jnp.broadcast_to(mods[bsz, k * D_MODEL:(k + 1) * D_MODEL][None, None, :], (bsz, 1, D_MODEL))
                 for k in range(6)]

        wl = w_in[l]
        w_da_qk = wl[:, :2 * DA_QK_COLS].astype(BF16)
        w_da_vt = wl[:, 2 * DA_QK_COLS:o_da].T.astype(BF16)
        w_ssd = jnp.pad(wl[:, o_da:o_ssd], ((0, 0), (0, 1664 - COLS_SSD))).astype(BF16)
        w_mla = jnp.pad(wl[:, o_ssd:o_mla], ((0, 0), (0, 768 - COLS_MLA))).astype(BF16)
        w_hy = wl[:, o_mla:o_hy].astype(BF16)
        w_gate = wl[:, o_hy:].astype(BF16)
        wuq = mla_w_uq[l].reshape(MLA_Q_LORA, MLA_HEADS, MLA_NOPE + MLA_ROPE)
        wuq = jnp.pad(wuq, ((0, 0), (0, 0), (0, 256 - MLA_NOPE - MLA_ROPE))).reshape(MLA_Q_LORA, 1024).astype(BF16)
        wukv = mla_w_ukv[l].reshape(MLA_KV_LORA, MLA_HEADS, MLA_NOPE + MLA_V)
        wuk = wukv[:, :, :MLA_NOPE].reshape(MLA_KV_LORA, MLA_HEADS * MLA_NOPE).astype(BF16)
        wuvt = wukv[:, :, MLA_NOPE:].reshape(MLA_KV_LORA, MLA_WIDTH).T.astype(BF16)
        g_pre = row(norm_mix_pre[l])

        lv = da_lambda[l].astype(F32)
        lam = (jnp.exp(jnp.sum(lv[0] * lv[1])) - jnp.exp(jnp.sum(lv[2] * lv[3])) + lam_init).reshape(1, 1)
        da_fin = (lam, da_subln[l].reshape(-1, 1).astype(F32), lam_init)

        dtb = jnp.pad(ssd_dt_bias[l].reshape(1, -1), ((0, 0), (0, LANES - 2 * SSD_HEADS))).astype(F32)
        a_row = jnp.pad(-jnp.exp(ssd_a_log[l].astype(F32)).reshape(1, -1), ((0, 0), (0, LANES - 2 * SSD_HEADS)))
        d_row = row(jnp.repeat(ssd_d[l], SSD_HEAD_DIM))
        g_ssd = row(ssd_norm[l])

        lp = {'hy_w1': hy_w1[l], 'hy_b1': hy_b1[l], 'hy_freq1': hy_freq1[l], 'hy_w2': hy_w2[l],
              'hy_b2': hy_b2[l], 'hy_freq2': hy_freq2[l], 'hy_w3': hy_w3[l]}
        hy_bias_l = hy_bias[l].astype(F32)

        need_ctx = not last
        streams = [(x_lat, m_lat, rope_tabs), (x_ctx, m_ctx, None)]
        proj = []
        for xs, ms, rt in streams:
            q_da, k_da, v_da = _inproj_da(xs, g_pre, ms[0], ms[1], w_da_qk, w_da_vt, rt)
            z_s, xbc, dt_raw = _inproj_ssd(xs, g_pre, ms[0], ms[1], w_ssd)
            q_m, k_m, v_m = _inproj_mla(xs, g_pre, ms[0], ms[1], w_mla, row(mla_q_norm[l]), wuq,
                                        row(mla_kv_norm[l]), wuk, wuvt, rt)
            proj.append(dict(q_da=q_da, k_da=k_da, v_da=v_da, z=z_s, xbc=xbc, dt=dt_raw, q_m=q_m, k_m=k_m, v_m=v_m))
        pl_, pc_ = proj

        heads_first = lambda k, h: k.reshape(bsz, k.shape[1], h, -1).transpose(0, 2, 1, 3)
        kc_da, kc_m = heads_first(pc_['k_da'], DA_HEADS), heads_first(pc_['k_m'], MLA_HEADS)
        k_da = jnp.concatenate([heads_first(pl_['k_da'], DA_HEADS), kc_da], axis=2)
        k_m = jnp.concatenate([heads_first(pl_['k_m'], MLA_HEADS), kc_m], axis=2)
        vt_da = jnp.concatenate([pl_['v_da'], pc_['v_da']], axis=2)
        vt_m = jnp.concatenate([pl_['v_m'], pc_['v_m']], axis=2)
        da_lat = _flash(pl_['q_da'], k_da, vt_da, ncomp=2, heads=DA_HEADS, dq=LANES, da=da_fin)
        mla_lat = _flash(pl_['q_m'], k_m, vt_m, ncomp=1, heads=MLA_HEADS, dq=256)
        cw, cb_ = ssd_conv_w[l].astype(F32), row(ssd_conv_b[l])
        xc_c = _dwconv(pc_['xbc'], cw, cb_, 1, True)
        xc_l = _dwconv(pl_['xbc'], cw, cb_, 1, True)
        zeros = jnp.zeros((bsz, 4, 128, 128), F32)
        yf_c, sf = _ssd_scan(xc_c, pc_['dt'], zeros, dtb, a_row, False)
        yf_l, _ = _ssd_scan(xc_l, pl_['dt'], sf, dtb, a_row, False)
        ssd_ctx, sb = _ssd_scan(xc_c, pc_['dt'], zeros, dtb, a_row, True, fin=(yf_c, pc_['z'], d_row, g_ssd))
        ssd_lat, _ = _ssd_scan(xc_l, pl_['dt'], sb, dtb, a_row, True, fin=(yf_l, pl_['z'], d_row, g_ssd))
        hw, hb = hy_conv_w[l].astype(F32), row(hy_conv_b[l])
        u_l = _dwconv(_inproj_hy(x_lat, g_pre, m_lat[0], m_lat[1], w_hy), hw, hb, 3, False)
        spec_l = _hy_spectrum(_hyena_filter(n_lat, lp), dft_tabs)
        u5 = u_l.reshape(3, bsz, n_lat // DFT_B, DFT_B, HY_WIDTH)
        z1 = _hy_long_conv(u5, 0, u5, 1, spec_l[0], hy_bias_l[0:1], dft_tabs)
        hy_lat = _hy_long_conv(z1[None], 0, u5, 2, spec_l[1], hy_bias_l[1:2], dft_tabs).reshape(bsz, n_lat, HY_WIDTH)

        wbr = jnp.stack([w_br_da[l], w_br_ssd[l], w_br_mla[l], w_br_hy[l]]).astype(BF16)
        wout = w_out[l].astype(BF16)
        g_post = row(norm_mix_post[l])
        w1, w3, w2 = ffn_w1[l].astype(BF16), ffn_w3[l].astype(BF16), ffn_w2[l].astype(BF16)
        gf_pre, gf_post = row(norm_ffn_pre[l]), row(norm_ffn_post[l])

        x_lat = _merge(x_lat, g_pre, m_lat[0], m_lat[1], (da_lat, ssd_lat, mla_lat, hy_lat), w_gate, wbr, wout,
                       g_post, m_lat[2])
        x_lat = _ffn(x_lat, gf_pre, m_lat[3], m_lat[4], w1, w3, w2, gf_post, m_lat[5])

        if need_ctx:
            da_ctx = _flash(pc_['q_da'], kc_da, pc_['v_da'], ncomp=2, heads=DA_HEADS, dq=LANES, da=da_fin)
            mla_ctx = _flash(pc_['q_m'], kc_m, pc_['v_m'], ncomp=1, heads=MLA_HEADS, dq=256)
            u_c = _dwconv(_inproj_hy(x_ctx, g_pre, m_ctx[0], m_ctx[1], w_hy), hw, hb, 3, False)
            g_c = _hyena_filter(n_ctx, lp).transpose(2, 1, 0, 3).reshape(2 * n_ctx, HY_ORDER * HY_WIDTH)
            spec_c = _ctx_spectrum(g_c, ctx_tabs['fw_full'])
            hy_ctx = _hy_ctx(u_c, spec_c, ctx_tabs['fw'], ctx_tabs['iv'], hy_bias_l)
            x_ctx = _merge(x_ctx, g_pre, m_ctx[0], m_ctx[1], (da_ctx, ssd_ctx, mla_ctx, hy_ctx), w_gate, wbr, wout,
                           g_post, m_ctx[2])
            x_ctx = _ffn(x_ctx, gf_pre, m_ctx[3], m_ctx[4], w1, w3, w2, gf_post, m_ctx[5])
    return x_lat
```

```python
import functools
import math

import jax
import jax.numpy as jnp
import numpy as np
from jax import lax
from jax.experimental import pallas as pl
from jax.experimental.pallas import tpu as pltpu

F32 = jnp.float32
BF16 = jnp.bfloat16
HIGHEST = lax.Precision.HIGHEST

D_MODEL = 1024
DEPTH = 2
GRID_W = 64
EPS = 1e-6
ROPE_BASE = 10000.0
ROT_DIM = 64

DA_HEADS = 4
DA_HEAD_DIM = 64
DA_V_DIM = 128
DA_QK_COLS = 512
DA_WIDTH = 512
COLS_DA = 1536

SSD_HEADS = 8
SSD_HEAD_DIM = 64
SSD_INNER = 512
SSD_GROUPS = 2
SSD_STATE = 128
SSD_CONV = 5
SSD_CHUNK = 128
SSD_XBC = 1024
COLS_SSD = 1552

MLA_HEADS = 4
MLA_Q_LORA = 384
MLA_KV_LORA = 256
MLA_NOPE = 128
MLA_ROPE = 64
MLA_V = 128
MLA_WIDTH = 512
COLS_MLA = 704

HY_WIDTH = 512
HY_ORDER = 2
HY_SHORT = 3
HY_EMB = 33
HY_TARGET = 1e-2
HY_DECAY_SHORT = 0.3
HY_DECAY_LONG = 1.5
COLS_HY = 1536

N_BRANCH = 4
COLS_GATE = 4096
FFN_HIDDEN = 2816

LANES = 128
SUBLANES = 8
VMEM_LIMIT_BYTES = 48 * 2 ** 20
NEG_BIG = -1e30
LOG2_E = 1.4426950408889634

ROW_TILE = 512
ATTN_TQ = 1024
ATTN_TK_MAX = 2816
ATTN_PROBE = 256
ATTN_SUM_MAX = 2.0 ** 64
FFN_TH = 768
DFT_B = 128


def _params(sem):
    return pltpu.CompilerParams(dimension_semantics=sem, vmem_limit_bytes=VMEM_LIMIT_BYTES)


def _full(shape):
    nd = len(shape)
    return pl.BlockSpec(shape, lambda *_: (0,) * nd)


def _resident(shape):
    nd = len(shape)
    return pl.BlockSpec(shape, lambda *_: (0,) * nd, pipeline_mode=pl.Buffered(1))


def _dot(a, b):
    return jnp.dot(a, b, preferred_element_type=F32)


def _dot_nt(a, b):
    return lax.dot_general(a, b, (((1,), (1,)), ((), ())), preferred_element_type=F32)


def _dot_tn(a, b):
    return lax.dot_general(a, b, (((0,), (0,)), ((), ())), preferred_element_type=F32)


def _dot_hi(a, b):
    return jnp.dot(a, b, precision=HIGHEST, preferred_element_type=F32)


def _rms(x, g):
    return x * lax.rsqrt(jnp.mean(x * x, axis=-1, keepdims=True) + EPS) * g


def _norm_mod(x, g, shift, scale):
    return _rms(x, g) * (1.0 + scale) + shift


def _silu(x):
    return x * jax.nn.sigmoid(x)


def _rope128(x, cosf, sins, first_half):
    partner = jnp.where(first_half, pltpu.roll(x, 96, 1), pltpu.roll(x, 32, 1))
    return x * cosf + partner * sins


def _first_half_mask(rows):
    return (lax.broadcasted_iota(jnp.int32, (rows, LANES), 1) & 32) == 0


def _mod_kernel(c_ref, w_ref, b_ref, o_ref):
    act = _silu(c_ref[...]).astype(BF16)
    o_ref[...] = _dot(act, w_ref[...]) + b_ref[...]


def _modulation(c_rows, w, b):
    n = w.shape[1] // D_MODEL
    return pl.pallas_call(
        _mod_kernel,
        grid=(n,),
        in_specs=[_full((SUBLANES, D_MODEL)),
                  pl.BlockSpec((D_MODEL, D_MODEL), lambda j: (0, j)),
                  pl.BlockSpec((1, D_MODEL), lambda j: (0, j))],
        out_specs=pl.BlockSpec((SUBLANES, D_MODEL), lambda j: (0, j)),
        out_shape=jax.ShapeDtypeStruct((SUBLANES, w.shape[1]), F32),
        compiler_params=_params(("parallel",)),
        name="modulation",
    )(c_rows, w, b)


INPROJ_TILE = 512


def _inproj_da_part(h, rope_cs, w_ref, wvt_ref, q_ref, k_ref, vt_ref):
    first = _first_half_mask(h.shape[0])
    scale = DA_HEAD_DIM ** -0.5 * LOG2_E
    for out_ref, c0, is_q in ((q_ref, 0, True), (k_ref, 512, False)):
        res = _dot(h, w_ref[:, c0:c0 + 512])
        for i in range(4):
            t = res[:, i * LANES:(i + 1) * LANES]
            if rope_cs is not None:
                t = _rope128(t, rope_cs[0], rope_cs[1], first)
            if is_q:
                t = t * scale
            out_ref[0, :, i * LANES:(i + 1) * LANES] = t.astype(BF16)
    vt_ref[0] = _dot_nt(wvt_ref[...], h).astype(BF16)


def _inproj_mla_part(h, rope_cs, w_ref, gq_ref, wuq_ref, gkv_ref, wuk_ref, wuvt_ref, q_ref, k_ref, vt_ref):
    rope = rope_cs is not None
    first = _first_half_mask(h.shape[0])
    res = _dot(h, w_ref[...])
    cq = _rms(res[:, :MLA_Q_LORA], gq_ref[...]).astype(BF16)
    ckv = _rms(res[:, MLA_Q_LORA:MLA_Q_LORA + MLA_KV_LORA], gkv_ref[...]).astype(BF16)
    kr = res[:, MLA_Q_LORA + MLA_KV_LORA:]
    q = _dot(cq, wuq_ref[...])
    kn = _dot(ckv, wuk_ref[...])
    if rope:
        kr = _rope128(kr, rope_cs[0], rope_cs[1], first)
    kr = kr.astype(BF16)
    scale = (MLA_NOPE + MLA_ROPE) ** -0.5 * LOG2_E
    for hh in range(MLA_HEADS):
        c0 = hh * 256
        qr = q[:, c0 + LANES:c0 + 256]
        if rope:
            qr = _rope128(qr, rope_cs[0], rope_cs[1], first)
        q_ref[0, :, c0:c0 + LANES] = (q[:, c0:c0 + LANES] * scale).astype(BF16)
        q_ref[0, :, c0 + LANES:c0 + 256] = (qr * scale).astype(BF16)
        k_ref[0, :, c0:c0 + LANES] = kn[:, hh * LANES:(hh + 1) * LANES].astype(BF16)
        k_ref[0, :, c0 + LANES:c0 + 256] = kr
    vt_ref[0] = _dot_nt(wuvt_ref[...], ckv).astype(BF16)


def _inproj_kernel(*refs, rope):
    x_ref, g_ref, sh_ref, sc_ref = refs[:4]
    (w_da, w_da_vt, w_ssd, w_mla, gq, wuq, gkv, wuk, wuvt, w_hy) = refs[4:14]
    pos = 14
    rope_cs = None
    if rope:
        rope_cs = (refs[14][...], refs[15][...])
        pos = 16
    q_da, k_da, z_ref, xbc_ref, dt_ref, q_m, k_m, hy_ref, vt_da, vt_m = refs[pos:pos + 10]
    h = _norm_mod(x_ref[0], g_ref[...], sh_ref[0], sc_ref[0]).astype(BF16)
    _inproj_da_part(h, rope_cs, w_da, w_da_vt, q_da, k_da, vt_da)
    z_ref[0] = _dot(h, w_ssd[:, 0:512])
    xbc_ref[0] = _dot(h, w_ssd[:, 512:1536])
    dt_ref[0] = _dot(h, w_ssd[:, 1536:1664])
    _inproj_mla_part(h, rope_cs, w_mla, gq, wuq, gkv, wuk, wuvt, q_m, k_m, vt_m)
    for c0 in range(0, COLS_HY, 512):
        hy_ref[0, :, c0:c0 + 512] = _dot(h, w_hy[:, c0:c0 + 512])


def _inproj(x, g, shift, scale, weights, rope_tabs):
    b, l, _ = x.shape
    tm = min(INPROJ_TILE, l)
    rope = rope_tabs is not None
    row = lambda d: pl.BlockSpec((1, tm, d), lambda bi, i: (bi, i, 0))
    vec = pl.BlockSpec((1, 1, D_MODEL), lambda bi, i: (bi, 0, 0))
    tok_last = pl.BlockSpec((1, 512, tm), lambda bi, i: (bi, 0, i))
    names = ('q_da', 'k_da', 'z', 'xbc', 'dt', 'q_m', 'k_m', 'hy')
    dims = (512, 512, 512, 1024, LANES, 1024, 1024, COLS_HY)
    dtypes = (BF16, BF16, F32, F32, F32, BF16, BF16, F32)
    in_specs = [row(D_MODEL), _full((1, D_MODEL)), vec, vec] + [_resident(w.shape) for w in weights]
    args = [x, g, shift, scale, *weights]
    if rope:
        in_specs += [pl.BlockSpec((tm, LANES), lambda bi, i: (i, 0))] * 2
        args += list(rope_tabs)
    outs = pl.pallas_call(
        functools.partial(_inproj_kernel, rope=rope),
        grid=(b, l // tm),
        in_specs=in_specs,
        out_specs=[row(d) for d in dims] + [tok_last, tok_last],
        out_shape=[jax.ShapeDtypeStruct((b, l, d), dt) for d, dt in zip(dims, dtypes)]
                  + [jax.ShapeDtypeStruct((b, 512, l), BF16)] * 2,
        compiler_params=_params(("parallel", "parallel")),
        name="inproj",
    )(*args)
    return dict(zip(names + ('v_da', 'v_m'), outs))


def _flash_kernel(*refs, ncomp, da_finish, lam_init):
    q_ref, k_ref, vt_ref = refs[:3]
    pos = 3
    if da_finish:
        lam_ref, sub_ref = refs[pos:pos + 2]
        pos += 2
    o_ref, m_ref, l_ref, acc_ref = refs[pos:pos + 4]
    j = pl.program_id(3)
    nk = pl.num_programs(3)
    q = q_ref[0]
    tq = q.shape[0]
    if ncomp == 2:
        lo = lax.broadcasted_iota(jnp.int32, (tq, LANES), 1) < DA_HEAD_DIM
        zero = jnp.zeros_like(q)
        qs = (jnp.where(lo, q, zero), jnp.where(lo, zero, q))
    else:
        qs = (q,)

    @pl.when(j == 0)
    def _():
        for c in range(ncomp):
            m_ref[c] = jnp.max(_dot_nt(k_ref[0, 0, :ATTN_PROBE, :], qs[c]), axis=0, keepdims=True)
        l_ref[...] = jnp.zeros(l_ref.shape, F32)
        acc_ref[...] = jnp.zeros(acc_ref.shape, F32)

    k = k_ref[0, 0]
    vt = vt_ref[0]
    for c in range(ncomp):
        m = m_ref[c]
        p = jnp.exp2(_dot_nt(k, qs[c]) - m)
        l_add = jnp.sum(p, axis=0, keepdims=True)
        pv = _dot(vt, p.astype(BF16))
        safe = jnp.max(l_add) <= ATTN_SUM_MAX

        @pl.when(safe)
        def _():
            l_ref[c] += l_add
            acc_ref[c] += pv

        @pl.when(jnp.logical_not(safe))
        def _():
            st = _dot_nt(k, qs[c])
            m_new = jnp.maximum(m, jnp.max(st, axis=0, keepdims=True))
            alpha = jnp.exp2(m - m_new)
            p2 = jnp.exp2(st - m_new)
            l_ref[c] = alpha * l_ref[c] + jnp.sum(p2, axis=0, keepdims=True)
            acc_ref[c] = alpha * acc_ref[c] + _dot(vt, p2.astype(BF16))
            m_ref[c] = m_new

    @pl.when(j == nk - 1)
    def _():
        o = acc_ref[0] / l_ref[0]
        if da_finish:
            o = o - lam_ref[...] * (acc_ref[1] / l_ref[1])
            ms = jnp.mean(o * o, axis=0, keepdims=True)
            o = o * lax.rsqrt(ms + EPS) * sub_ref[...] * (1.0 - lam_init)
        o_ref[0] = o.T.astype(o_ref.dtype)


def _key_tile(lk):
    best = LANES
    for t in range(LANES, min(lk, ATTN_TK_MAX) + 1, LANES):
        if lk % t == 0:
            best = t
    return best


def _flash(q, k, vt, *, ncomp, heads, dq, da=None):
    b, lq, _ = q.shape
    lk = k.shape[2]
    tq = min(ATTN_TQ, lq)
    nq = lq // tq
    tk = _key_tile(lk)
    nk = lk // tk
    in_specs = [pl.BlockSpec((1, tq, dq), lambda bi, h, i, j: (bi, i, h)),
                pl.BlockSpec((1, 1, tk, dq), lambda bi, h, i, j: (bi, h, j, 0)),
                pl.BlockSpec((1, LANES, tk), lambda bi, h, i, j: (bi, h, j))]
    args = [q, k, vt]
    lam_init = 0.0
    if da is not None:
        lam, subln, lam_init = da
        in_specs += [_full((1, 1)), _full((LANES, 1))]
        args += [lam, subln]
    kernel = functools.partial(_flash_kernel, ncomp=ncomp, da_finish=da is not None, lam_init=lam_init)
    return pl.pallas_call(
        kernel,
        grid=(b, heads, nq, nk),
        in_specs=in_specs,
        out_specs=pl.BlockSpec((1, tq, LANES), lambda bi, h, i, j: (bi, i, h)),
        out_shape=jax.ShapeDtypeStruct((b, lq, heads * LANES), BF16),
        scratch_shapes=[pltpu.VMEM((ncomp, 1, tq), F32), pltpu.VMEM((ncomp, 1, tq), F32),
                        pltpu.VMEM((ncomp, LANES, tq), F32)],
        compiler_params=_params(("parallel", "parallel", "parallel", "arbitrary")),
        name="flash_da" if ncomp == 2 else "flash_mla",
    )(*args)


def _dwconv_kernel(x_ref, w_ref, b_ref, o_ref, pad_ref, *, taps, act, chunk):
    l = x_ref.shape[1]
    pad_ref[0:SUBLANES, :] = jnp.zeros((SUBLANES, LANES), F32)
    pad_ref[SUBLANES + l:2 * SUBLANES + l, :] = jnp.zeros((SUBLANES, LANES), F32)
    pad_ref[SUBLANES:SUBLANES + l, :] = x_ref[0]
    w = w_ref[...]
    bias = b_ref[...]

    def body(i, carry):
        base = pl.multiple_of(i * chunk, chunk)
        acc = jnp.zeros((chunk, LANES), F32) + bias
        for t in range(taps):
            acc = acc + w[t:t + 1, :] * pad_ref[pl.ds(base + SUBLANES + t - taps // 2, chunk), :]
        if act:
            acc = _silu(acc)
        o_ref[0, 0, pl.ds(base, chunk), :] = acc
        return carry

    lax.fori_loop(0, l // chunk, body, 0)


def _dwconv(x, w, bias, groups, act):
    b, l, c = x.shape
    taps = w.shape[0]
    cg = c // groups
    per = cg // LANES
    chunk = min(512, l)
    return pl.pallas_call(
        functools.partial(_dwconv_kernel, taps=taps, act=act, chunk=chunk),
        grid=(b, c // LANES),
        in_specs=[pl.BlockSpec((1, l, LANES), lambda bi, ct: (bi, 0, ct)),
                  pl.BlockSpec((taps, LANES), lambda bi, ct: (0, ct)),
                  pl.BlockSpec((1, LANES), lambda bi, ct: (0, ct))],
        out_specs=pl.BlockSpec((1, 1, l, LANES), lambda bi, ct: (ct // per, bi, 0, ct % per)),
        out_shape=jax.ShapeDtypeStruct((groups, b, l, cg), F32),
        scratch_shapes=[pltpu.VMEM((l + 2 * SUBLANES, LANES), F32)],
        compiler_params=_params(("parallel", "parallel")),
        name="dwconv",
    )(x, w, bias)


SSD_STEP_CHUNKS = 4

def _ssd_kernel(*refs, reverse, finish):
    if finish:
        (xc_ref, dt_ref, init_ref, dtb_ref, a_ref, yo_ref, z_ref, d_ref, g_ref,
         y_ref, fin_ref, st_ref) = refs
    else:
        xc_ref, dt_ref, init_ref, dtb_ref, a_ref, y_ref, fin_ref, st_ref = refs
    j = pl.program_id(1)
    nc = pl.num_programs(1)
    q = SSD_CHUNK

    @pl.when(j == 0)
    def _():
        st_ref[...] = init_ref[0]

    r = lax.broadcasted_iota(jnp.int32, (q, q), 0)
    c = lax.broadcasted_iota(jnp.int32, (q, q), 1)
    tri = (c >= r) if reverse else (c <= r)
    off = SSD_HEADS if reverse else 0
    lane_lo = c < SSD_HEAD_DIM
    row_lo = r < SSD_HEAD_DIM
    n_sub = xc_ref.shape[2] // q
    for sub in (range(n_sub - 1, -1, -1) if reverse else range(n_sub)):
        rows = slice(sub * q, (sub + 1) * q)
        _ssd_chunk(xc_ref[0, 0, rows, :], dt_ref[0, rows, :], dtb_ref, a_ref, st_ref, y_ref, rows, tri, off,
                   lane_lo, row_lo, reverse,
                   (yo_ref[0, rows, :], z_ref[0, rows, :], d_ref, g_ref) if finish else None)

    @pl.when(j == nc - 1)
    def _():
        fin_ref[0] = st_ref[...]


def _ssd_chunk(xc, dt_raw, dtb_ref, a_ref, st_ref, y_ref, rows, tri, off, lane_lo, row_lo, reverse, fin):
    q = SSD_CHUNK
    finish = fin is not None
    pre = dt_raw + dtb_ref[...]
    dt = jnp.maximum(pre, 0.0) + jnp.log1p(jnp.exp(-jnp.abs(pre)))
    a = dt * a_ref[...]
    cum = _dot_hi(tri.astype(F32), a)
    cum_t = cum.T
    tot = cum[0:1, :] if reverse else cum[q - 1:q, :]
    ys = []
    for g in range(SSD_GROUPS):
        bm = xc[:, 512 + g * 128:512 + (g + 1) * 128].astype(BF16)
        cm = xc[:, 768 + g * 128:768 + (g + 1) * 128].astype(BF16)
        cb = _dot_nt(cm, bm)
        for pp in range(2):
            hp = 2 * g + pp
            l0 = off + 2 * hp
            l1 = l0 + 1
            col0, col1 = cum[:, l0:l0 + 1], cum[:, l1:l1 + 1]
            dec0 = jnp.exp(jnp.where(tri, col0 - cum_t[l0:l0 + 1, :], NEG_BIG))
            dec1 = jnp.exp(jnp.where(tri, col1 - cum_t[l1:l1 + 1, :], NEG_BIG))
            xdt = xc[:, hp * 128:(hp + 1) * 128] * jnp.where(lane_lo, dt[:, l0:l0 + 1], dt[:, l1:l1 + 1])
            xdt_b = xdt.astype(BF16)
            zero = jnp.zeros_like(xdt_b)
            y = (_dot((cb * dec0).astype(BF16), jnp.where(lane_lo, xdt_b, zero))
                 + _dot((cb * dec1).astype(BF16), jnp.where(lane_lo, zero, xdt_b)))
            colp = jnp.where(lane_lo, col0, col1)
            totp = jnp.where(lane_lo, tot[:, l0:l0 + 1], tot[:, l1:l1 + 1])
            contrib = _dot_tn((xdt * jnp.exp(totp - colp)).astype(BF16), bm)
            prev = st_ref[hp]
            y = y + _dot_nt(cm, prev.astype(BF16)) * jnp.exp(colp)
            st_ref[hp] = prev * jnp.exp(jnp.where(row_lo, tot[:, l0:l0 + 1], tot[:, l1:l1 + 1])) + contrib
            if finish:
                ys.append(y)
            else:
                y_ref[0, rows, hp * 128:(hp + 1) * 128] = y

    if finish:
        yo, z, d_ref, g_ref = fin
        yt = jnp.concatenate(ys, axis=1) + yo + d_ref[...] * xc[:, :SSD_INNER]
        y_ref[0, rows, :] = _rms(yt * _silu(z), g_ref[...]).astype(y_ref.dtype)


def _ssd_scan(xc, dt_raw, init, dtb, a_row, reverse, fin=None):
    _, b, l, _ = xc.shape
    step = SSD_CHUNK * min(SSD_STEP_CHUNKS, l // SSD_CHUNK)
    nc = l // step
    cidx = (lambda j: nc - 1 - j) if reverse else (lambda j: j)
    rows = lambda d: pl.BlockSpec((1, step, d), lambda bi, j: (bi, cidx(j), 0))
    st_spec = pl.BlockSpec((1, 4, 128, 128), lambda bi, j: (bi, 0, 0, 0))
    in_specs = [pl.BlockSpec((1, 1, step, SSD_XBC), lambda bi, j: (0, bi, cidx(j), 0)),
                rows(LANES), st_spec, _full((1, LANES)), _full((1, LANES))]
    args = [xc, dt_raw, init, dtb, a_row]
    finish = fin is not None
    if finish:
        y_other, z, d_row, g = fin
        in_specs += [rows(SSD_INNER), rows(SSD_INNER), _full((1, SSD_INNER)), _full((1, SSD_INNER))]
        args += [y_other, z, d_row, g]
    return pl.pallas_call(
        functools.partial(_ssd_kernel, reverse=reverse, finish=finish),
        grid=(b, nc),
        in_specs=in_specs,
        out_specs=[rows(SSD_INNER), st_spec],
        out_shape=[jax.ShapeDtypeStruct((b, l, SSD_INNER), BF16 if finish else F32),
                   jax.ShapeDtypeStruct((b, 4, 128, 128), F32)],
        scratch_shapes=[pltpu.VMEM((4, 128, 128), F32)],
        compiler_params=_params(("parallel", "arbitrary")),
        name="ssd_bwd" if reverse else "ssd_fwd",
    )(*args)


def _split_bf16(x):
    hi = x.astype(BF16)
    return hi, (x - hi.astype(F32)).astype(BF16)


def _dot_split(a3_ref, b):
    b_hi, b_lo = _split_bf16(b)
    return _dot(a3_ref[...], jnp.concatenate([b_hi, b_hi, b_lo], axis=0))


def _lane_tile(t, width):
    return jnp.concatenate([t] * (width // LANES), axis=1)


def _hy_dft_a_kernel(z_ref, fa_ref, tw_ref, o_ref, *, pair):
    a8, c = o_ref.shape[2], o_ref.shape[5]
    for i in range(SUBLANES):
        if pair:
            z = jnp.concatenate([z_ref[0, 0, :, i, :], z_ref[0, 1, :, i, :]], axis=0)
        else:
            z = z_ref[0, 0, i]
        res = _dot_split(fa_ref, z)
        half = res.shape[0] // 2
        xr, xi = res[:half], res[half:]
        tr, ti = _lane_tile(tw_ref[i, 0], c), _lane_tile(tw_ref[i, 1], c)
        o_ref[0, 0, :, i] = (xr * tr - xi * ti).reshape(a8, SUBLANES, c)
        o_ref[0, 1, :, i] = (xr * ti + xi * tr).reshape(a8, SUBLANES, c)


def _hy_dft_a(z5, group, fa3, tw, pair):
    s, c = z5.shape[1], z5.shape[4]
    dft_a = tw.shape[2]
    if pair:
        p = s // 2
        z_spec = pl.BlockSpec((1, 2, z5.shape[2], SUBLANES, c), lambda pi, jb: (group, pi, 0, jb, 0))
    else:
        p = s
        z_spec = pl.BlockSpec((1, 1, SUBLANES, z5.shape[3], c), lambda pi, jb: (group, pi, jb, 0, 0))
    return pl.pallas_call(
        functools.partial(_hy_dft_a_kernel, pair=pair),
        grid=(p, DFT_B // SUBLANES),
        in_specs=[z_spec,
                  _full(fa3.shape),
                  pl.BlockSpec((SUBLANES, 2, dft_a, LANES), lambda pi, jb: (jb, 0, 0, 0))],
        out_specs=pl.BlockSpec((1, 2, dft_a // SUBLANES, SUBLANES, SUBLANES, c),
                               lambda pi, jb: (pi, 0, 0, jb, 0, 0)),
        out_shape=jax.ShapeDtypeStruct((p, 2, dft_a // SUBLANES, DFT_B, SUBLANES, c), F32),
        compiler_params=_params(("parallel", "parallel")),
        name="hyena_dft_a",
    )(z5, fa3, tw)


def _hy_dft_b_kernel(*refs, conv):
    if conv:
        y_ref, s_ref, mf_ref, mi_ref, o_ref = refs
    else:
        y_ref, mf_ref, scale_ref, o_ref = refs
    for kl in range(SUBLANES):
        y = jnp.concatenate([y_ref[0, 0, 0, :, kl, :], y_ref[0, 1, 0, :, kl, :]], axis=0)
        out = _dot_split(mf_ref, y)
        if conv:
            xr, xi = out[:DFT_B], out[DFT_B:]
            sr, si = s_ref[0, 0, :, kl, :], s_ref[1, 0, :, kl, :]
            out = _dot_split(mi_ref, jnp.concatenate([xr * sr - xi * si, xr * si + xi * sr], axis=0))
        else:
            out = out * scale_ref[0]
        o_ref[0, 0, 0, :, kl, :] = out[:DFT_B]
        o_ref[0, 1, 0, :, kl, :] = out[DFT_B:]


def _hy_dft_b(y1, mf3, spec_mi=None, scale=None):
    p, _, a8, _, _, c = y1.shape
    conv = spec_mi is not None
    blk = pl.BlockSpec((1, 2, 1, DFT_B, SUBLANES, c), lambda kh, pi: (pi, 0, kh, 0, 0, 0))
    in_specs, args = [blk], [y1]
    if conv:
        in_specs += [pl.BlockSpec((2, 1, DFT_B, SUBLANES, c), lambda kh, pi: (0, kh, 0, 0, 0)), _full(mf3.shape),
                     _full(spec_mi[1].shape)]
        args += [spec_mi[0], mf3, spec_mi[1]]
    else:
        in_specs += [_full(mf3.shape), pl.BlockSpec((1, 1, c), lambda kh, pi: (pi, 0, 0))]
        args += [mf3, scale]
    return pl.pallas_call(
        functools.partial(_hy_dft_b_kernel, conv=conv),
        grid=(a8, p),
        in_specs=in_specs,
        out_specs=blk,
        out_shape=jax.ShapeDtypeStruct(y1.shape, F32),
        compiler_params=_params(("parallel", "parallel")),
        name="hyena_dft_b" if conv else "hyena_spectrum_b",
    )(*args)


def _hy_dft_c_kernel(y_ref, tw_ref, fc_ref, gate_ref, zp_ref, bias_ref, o_ref):
    a8, c = y_ref.shape[2], y_ref.shape[5]
    na = o_ref.shape[1]
    bias = bias_ref[...]
    for i in range(SUBLANES):
        yr = y_ref[0, 0, :, i].reshape(a8 * SUBLANES, c)
        yi = y_ref[0, 1, :, i].reshape(a8 * SUBLANES, c)
        tr, ti = _lane_tile(tw_ref[i, 0], c), _lane_tile(tw_ref[i, 1], c)
        conv = _dot_split(fc_ref, jnp.concatenate([yr * tr + yi * ti, yi * tr - yr * ti], axis=0))
        for s in range(2):
            o_ref[s, :, i, :] = gate_ref[0, s, :, i, :] * (conv[s * na:(s + 1) * na] + bias * zp_ref[0, s, :, i, :])


def _hy_filter_kernel(hf_ref, hb_ref, tf_ref, tb_ref, w_ref, d_ref, g_ref, ss_ref):
    j = pl.program_id(1)
    rows, width = hf_ref.shape[0], g_ref.shape[3]
    na = rows // SUBLANES
    fwd = _dot(hf_ref[...].astype(BF16), w_ref[0, 0]) * jnp.exp(-tf_ref[...] * d_ref[...])
    bwd = _dot(hb_ref[...].astype(BF16), w_ref[1, 0]) * jnp.exp(-tb_ref[...] * d_ref[...])
    first = jnp.logical_and(lax.broadcasted_iota(jnp.int32, (rows, width), 0) == 0, j == 0)
    fwd = jnp.where(first, fwd + bwd, fwd)
    bwd = jnp.where(first, 0.0, bwd)
    g_ref[0, :, :na] = fwd.reshape(SUBLANES, na, width)
    g_ref[0, :, na:] = bwd.reshape(SUBLANES, na, width)
    ss = jnp.sum(fwd * fwd, axis=0, keepdims=True) + jnp.sum(bwd * bwd, axis=0, keepdims=True)

    @pl.when(j == 0)
    def _():
        ss_ref[0] = ss

    @pl.when(j > 0)
    def _():
        ss_ref[0] += ss


def _hy_filter(hid, t_unit, w3, deltas, n):
    na = n // DFT_B
    rows = SUBLANES * na
    order, hidden, width = w3.shape[1], w3.shape[2], w3.shape[3]
    lag_f = (DFT_B * np.arange(na)[None, :] + np.arange(DFT_B)[:, None]).reshape(-1)
    lag_b = n - lag_f
    lag_b[0] = 0
    hf, hb = hid[lag_f], hid[lag_b]
    tf, tb = t_unit[lag_f][:, None], t_unit[lag_b][:, None]
    rspec = lambda d: pl.BlockSpec((rows, d), lambda o, j: (j, 0))
    return pl.pallas_call(
        _hy_filter_kernel,
        grid=(order, DFT_B // SUBLANES),
        in_specs=[rspec(hidden), rspec(hidden), rspec(1), rspec(1),
                  pl.BlockSpec((2, 1, hidden, width), lambda o, j: (0, o, 0, 0)), _full((1, width))],
        out_specs=[pl.BlockSpec((1, SUBLANES, 2 * na, width), lambda o, j: (o, j, 0, 0)),
                   pl.BlockSpec((1, 1, width), lambda o, j: (o, 0, 0))],
        out_shape=[jax.ShapeDtypeStruct((order, DFT_B, 2 * na, width), F32),
                   jax.ShapeDtypeStruct((order, 1, width), F32)],
        compiler_params=_params(("parallel", "arbitrary")),
        name="hyena_filter",
    )(hf, hb, tf, tb, w3, deltas)


def _hy_spectrum(g, sumsq, tabs):
    y1 = _hy_dft_a(g[None], 0, tabs['fa_full'], tabs['tw'], False)
    return _hy_dft_b(y1, tabs['mf'], scale=lax.rsqrt(sumsq + EPS))


def _hy_long_conv(z5, zgroup, u5, gate_group, spec_l, bias_row, tabs):
    fc, tw = tabs['fc'], tabs['tw']
    _, b, na, _, c = z5.shape
    dft_a = 2 * na
    y1 = _hy_dft_a(z5, zgroup, tabs['fa'], tw, True)
    y2 = _hy_dft_b(y1, tabs['mf'], (spec_l, tabs['mi']))
    seq = lambda g: pl.BlockSpec((1, 2, na, SUBLANES, c), lambda pi, jb: (g, pi, 0, jb, 0))
    return pl.pallas_call(
        _hy_dft_c_kernel,
        grid=(b // 2, DFT_B // SUBLANES),
        in_specs=[pl.BlockSpec((1, 2, dft_a // SUBLANES, SUBLANES, SUBLANES, c), lambda pi, jb: (pi, 0, 0, jb, 0, 0)),
                  pl.BlockSpec((SUBLANES, 2, dft_a, LANES), lambda pi, jb: (jb, 0, 0, 0)),
                  _full(fc.shape), seq(gate_group), seq(zgroup), _full((1, c))],
        out_specs=pl.BlockSpec((2, na, SUBLANES, c), lambda pi, jb: (pi, 0, jb, 0)),
        out_shape=jax.ShapeDtypeStruct((b, na, DFT_B, c), F32),
        compiler_params=_params(("parallel", "parallel")),
        name="hyena_dft_c",
    )(y2, tw, fc, u5, z5, bias_row)


def _hy_ctx_kernel(v_ref, x1_ref, x2_ref, s_ref, fw_ref, iv_ref, bias_ref, o_ref):
    z = v_ref[0, 0]
    n2 = fw_ref.shape[0] // 2
    for o, gate_ref in enumerate((x1_ref, x2_ref)):
        x = _dot_split(fw_ref, z)
        xr, xi = x[:n2], x[n2:]
        sr, si = s_ref[0, :, o * HY_WIDTH:(o + 1) * HY_WIDTH], s_ref[1, :, o * HY_WIDTH:(o + 1) * HY_WIDTH]
        prod = jnp.concatenate([xr * sr - xi * si, xr * si + xi * sr], axis=0)
        conv = _dot_split(iv_ref, prod)
        z = gate_ref[0, 0] * (conv + bias_ref[o:o + 1, :] * z)
    o_ref[0] = z.astype(o_ref.dtype)


def _ctx_spectrum_kernel(g_ref, f_ref, o_ref):
    res = _dot_split(f_ref, g_ref[...])
    half = res.shape[0] // 2
    o_ref[0] = res[:half]
    o_ref[1] = res[half:]


def _ctx_spectrum(g, f3):
    n2, c = g.shape
    return pl.pallas_call(
        _ctx_spectrum_kernel,
        in_specs=[_full(g.shape), _full(f3.shape)],
        out_specs=_full((2, n2, c)),
        out_shape=jax.ShapeDtypeStruct((2, n2, c), F32),
        compiler_params=pltpu.CompilerParams(vmem_limit_bytes=VMEM_LIMIT_BYTES),
        name="hyena_ctx_spectrum",
    )(g, f3)


def _hy_ctx(u4, spec_c, fw, iv, bias):
    _, b, n, c = u4.shape
    grp = lambda g: pl.BlockSpec((1, 1, n, c), lambda bi: (g, bi, 0, 0))
    return pl.pallas_call(
        _hy_ctx_kernel,
        grid=(b,),
        in_specs=[grp(0), grp(1), grp(2), _full(spec_c.shape), _full(fw.shape), _full(iv.shape), _full(bias.shape)],
        out_specs=pl.BlockSpec((1, n, c), lambda bi: (bi, 0, 0)),
        out_shape=jax.ShapeDtypeStruct((b, n, c), BF16),
        compiler_params=_params(("parallel",)),
        name="hyena_ctx",
    )(u4, u4, u4, spec_c, fw, iv, bias)


def _merge_kernel(x_ref, g_ref, sh_ref, sc_ref, da_ref, ssd_ref, mla_ref, hy_ref, wgate_ref, wbr_ref, wout_ref,
                  gpost_ref, mg_ref, o_ref):
    x = x_ref[0]
    h = _norm_mod(x, g_ref[...], sh_ref[0], sc_ref[0]).astype(BF16)
    mixed = None
    for k, br_ref in enumerate((da_ref, ssd_ref, mla_ref, hy_ref)):
        gate = jax.nn.sigmoid(_dot(h, wgate_ref[:, k * D_MODEL:(k + 1) * D_MODEL]))
        t = gate * _dot(br_ref[0].astype(BF16), wbr_ref[k])
        mixed = t if mixed is None else mixed + t
    y = _dot(mixed.astype(BF16), wout_ref[...])
    o_ref[0] = x + mg_ref[0] * _rms(y, gpost_ref[...])


def _merge(x, g, shift, scale, branches, wgate, wbr, wout, gpost, mgate):
    b, l, _ = x.shape
    tm = min(ROW_TILE, l)
    row = lambda d: pl.BlockSpec((1, tm, d), lambda bi, i: (bi, i, 0))
    vec = pl.BlockSpec((1, 1, D_MODEL), lambda bi, i: (bi, 0, 0))
    return pl.pallas_call(
        _merge_kernel,
        grid=(b, l // tm),
        in_specs=[row(D_MODEL), _full((1, D_MODEL)), vec, vec] + [row(512)] * 4
                 + [_full(wgate.shape), _full(wbr.shape), _full(wout.shape), _full((1, D_MODEL)), vec],
        out_specs=row(D_MODEL),
        out_shape=jax.ShapeDtypeStruct((b, l, D_MODEL), F32),
        compiler_params=_params(("parallel", "parallel")),
        name="merge",
    )(x, g, shift, scale, *branches, wgate, wbr, wout, gpost, mgate)


def _ffn_kernel(x_ref, g_ref, sh_ref, sc_ref, w1_ref, w3_ref, w2_ref, gpost_ref, mg_ref, o_ref):
    x = x_ref[0]
    h = _norm_mod(x, g_ref[...], sh_ref[0], sc_ref[0]).astype(BF16)
    f = None
    for c0 in range(0, FFN_HIDDEN, FFN_TH):
        c1 = min(c0 + FFN_TH, FFN_HIDDEN)
        t = (_silu(_dot(h, w1_ref[:, c0:c1])) * _dot(h, w3_ref[:, c0:c1])).astype(BF16)
        part = _dot(t, w2_ref[c0:c1, :])
        f = part if f is None else f + part
    o_ref[0] = x + mg_ref[0] * _rms(f, gpost_ref[...])


def _ffn(x, g, shift, scale, w1, w3, w2, gpost, mgate):
    b, l, _ = x.shape
    tm = min(ROW_TILE, l)
    row = pl.BlockSpec((1, tm, D_MODEL), lambda bi, i: (bi, i, 0))
    vec = pl.BlockSpec((1, 1, D_MODEL), lambda bi, i: (bi, 0, 0))
    return pl.pallas_call(
        _ffn_kernel,
        grid=(b, l // tm),
        in_specs=[row, _full((1, D_MODEL)), vec, vec, _resident(w1.shape), _resident(w3.shape), _resident(w2.shape),
                  _full((1, D_MODEL)), vec],
        out_specs=row,
        out_shape=jax.ShapeDtypeStruct((b, l, D_MODEL), F32),
        compiler_params=_params(("parallel", "parallel")),
        name="ffn",
    )(x, g, shift, scale, w1, w3, w2, gpost, mgate)


def _rope_tables(n_lat):
    t = jnp.arange(n_lat)
    pos_row = (t // GRID_W).astype(F32)
    pos_col = (t % GRID_W).astype(F32)
    quarter = ROT_DIM // 4
    inv = ROPE_BASE ** (-jnp.arange(quarter, dtype=F32) / quarter)
    ang = jnp.concatenate([pos_row[:, None] * inv, pos_col[:, None] * inv], axis=-1)
    cos, sin = jnp.cos(ang), jnp.sin(ang)
    cosf = jnp.tile(cos, (1, 4))
    sins = jnp.tile(jnp.concatenate([-sin, sin], axis=-1), (1, 2))
    return cosf, sins


def _filter_features(n, lp):
    t = jnp.arange(n, dtype=F32)
    t_unit = t / (n - 1)
    bands = (HY_EMB - 1) // 2
    band_f = jnp.linspace(1e-4, bands - 1, bands, dtype=F32)
    w = 2.0 * math.pi * t / n
    feats = jnp.concatenate([t_unit[:, None], jnp.cos(w[:, None] * band_f), -jnp.sin(w[:, None] * band_f)], axis=-1)
    hid = jnp.sin(lp['hy_freq1'] * (feats @ lp['hy_w1'] + lp['hy_b1']))
    hid = jnp.sin(lp['hy_freq2'] * (hid @ lp['hy_w2'] + lp['hy_b2']))
    w3 = lp['hy_w3'].reshape(-1, 2, HY_ORDER, HY_WIDTH)
    deltas = jnp.abs(jnp.linspace(math.log(HY_TARGET) / HY_DECAY_LONG, math.log(HY_TARGET) / HY_DECAY_SHORT,
                                  HY_WIDTH, dtype=F32))
    return hid, t_unit, w3, deltas


def _hyena_filter(n, lp):
    na = n // DFT_B
    hid, t_unit, w3, deltas = _filter_features(n, lp)
    lag_f = DFT_B * np.arange(na)[None, :] + np.arange(DFT_B)[:, None]
    lag_b = n - lag_f
    zero_tap = jnp.asarray(lag_b == n)[:, :, None]
    lag_b = np.minimum(lag_b, n - 1)

    def taps(lags, w_dir):
        flat = lags.reshape(-1)
        vals = (hid[flat] @ w_dir) * jnp.exp(-t_unit[flat][:, None] * deltas)
        return vals.reshape(DFT_B, na, HY_WIDTH)

    orders = []
    for o in range(HY_ORDER):
        fwd = taps(lag_f, w3[:, 0, o])
        fwd = fwd.at[0, 0].add(hid[0] @ w3[:, 1, o])
        bwd = jnp.where(zero_tap, 0.0, taps(lag_b, w3[:, 1, o]))
        orders.append(jnp.concatenate([fwd, bwd], axis=1))
    g = jnp.stack(orders)
    return g * lax.rsqrt(jnp.sum(g * g, axis=(1, 2), keepdims=True) + EPS)


def _stack3(m):
    m32 = jnp.asarray(np.ascontiguousarray(m), F32)
    hi = m32.astype(BF16)
    lo = (m32 - hi.astype(F32)).astype(BF16)
    return jnp.concatenate([hi, lo, hi], axis=1)


def _dft_tables(n):
    na = n // DFT_B
    dft_a = 2 * na
    ka = np.arange(dft_a)[:, None]
    a = np.arange(dft_a)[None, :]
    th = 2.0 * np.pi * (ka * a % dft_a) / dft_a
    ca, sa = np.cos(th), np.sin(th)
    kb = np.arange(DFT_B)[:, None]
    tb = 2.0 * np.pi * (kb * kb.T % DFT_B) / DFT_B
    cb, sb = np.cos(tb), np.sin(tb)
    fa_full = np.concatenate([ca, -sa], axis=0)
    bb = np.arange(DFT_B)[:, None]
    kk = np.arange(dft_a)[None, :]
    ph = 2.0 * np.pi * (bb * kk) / (2 * n)
    tw = np.stack([np.cos(ph), -np.sin(ph)], axis=1)
    tw = np.broadcast_to(tw[..., None], (DFT_B, 2, dft_a, LANES))
    can, san = ca[:, :na], sa[:, :na]
    return dict(
        fa=_stack3(np.block([[can, san], [-san, can]])),
        fa_full=_stack3(fa_full),
        mf=_stack3(np.block([[cb, sb], [-sb, cb]])),
        mi=_stack3(np.block([[cb, -sb], [sb, cb]])),
        fc=_stack3(np.block([[can.T, -san.T], [san.T, can.T]]) / (2 * n)),
        tw=jnp.asarray(np.ascontiguousarray(tw), F32))


def _ctx_dft_tables(n):
    k = np.arange(2 * n)[:, None]
    t = np.arange(2 * n)[None, :]
    th = 2.0 * np.pi * (k * t % (2 * n)) / (2 * n)
    fw_full = np.concatenate([np.cos(th), -np.sin(th)], axis=0)
    iv = np.concatenate([np.cos(th)[:n], -np.sin(th)[:n]], axis=1) / (2 * n)
    return dict(fw=_stack3(fw_full[:, :n]), fw_full=_stack3(fw_full), iv=_stack3(iv))


def kernel(x, c, ctx, c_ctx, mod_w, mod_b, norm_mix_pre, norm_mix_post, norm_ffn_pre, norm_ffn_post, w_in,
           da_lambda, da_subln, ssd_conv_w, ssd_conv_b, ssd_a_log, ssd_dt_bias, ssd_d, ssd_norm, mla_q_norm,
           mla_w_uq, mla_kv_norm, mla_w_ukv, hy_conv_w, hy_conv_b, hy_w1, hy_b1, hy_freq1, hy_w2, hy_b2, hy_freq2,
           hy_w3, hy_bias, w_br_da, w_br_ssd, w_br_mla, w_br_hy, w_out, ffn_w1, ffn_w3, ffn_w2):
    bsz, n_lat, _ = x.shape
    n_ctx = ctx.shape[1]
    depth = w_in.shape[0]
    rope_tabs = _rope_tables(n_lat)
    dft_tabs = _dft_tables(n_lat)
    ctx_tabs = _ctx_dft_tables(n_ctx)
    row = lambda v: v.reshape(1, -1).astype(F32)

    c_rows = jnp.concatenate([c, c_ctx[None, :], jnp.zeros((SUBLANES - bsz - 1, D_MODEL), F32)], axis=0)
    o_da, o_ssd, o_mla, o_hy = np.cumsum([COLS_DA, COLS_SSD, COLS_MLA, COLS_HY]).tolist()

    x_lat, x_ctx = x, ctx
    for l in range(depth):
        last = l == depth - 1
        lam_init = 0.8 - 0.6 * math.exp(-0.3 * l)
        mods = _modulation(c_rows, mod_w[l].astype(BF16), row(mod_b[l]))
        m_lat = [mods[:bsz, k * D_MODEL:(k + 1) * D_MODEL][:, None, :] for k in range(6)]
        m_ctx = [jnp.broadcast_to(mods[bsz, k * D_MODEL:(k + 1) * D_MODEL][None, None, :], (bsz, 1, D_MODEL))
                 for k in range(6)]

        wl = w_in[l]
        w_da_qk = wl[:, :2 * DA_QK_COLS].astype(BF16)
        w_da_vt = wl[:, 2 * DA_QK_COLS:o_da].T.astype(BF16)
        w_ssd = jnp.pad(wl[:, o_da:o_ssd], ((0, 0), (0, 1664 - COLS_SSD))).astype(BF16)
        w_mla = jnp.pad(wl[:, o_ssd:o_mla], ((0, 0), (0, 768 - COLS_MLA))).astype(BF16)
        w_hy = wl[:, o_mla:o_hy].astype(BF16)
        w_gate = wl[:, o_hy:].astype(BF16)
        wuq = mla_w_uq[l].reshape(MLA_Q_LORA, MLA_HEADS, MLA_NOPE + MLA_ROPE)
        wuq = jnp.pad(wuq, ((0, 0), (0, 0), (0, 256 - MLA_NOPE - MLA_ROPE))).reshape(MLA_Q_LORA, 1024).astype(BF16)
        wukv = mla_w_ukv[l].reshape(MLA_KV_LORA, MLA_HEADS, MLA_NOPE + MLA_V)
        wuk = wukv[:, :, :MLA_NOPE].reshape(MLA_KV_LORA, MLA_HEADS * MLA_NOPE).astype(BF16)
        wuvt = wukv[:, :, MLA_NOPE:].reshape(MLA_KV_LORA, MLA_WIDTH).T.astype(BF16)
        g_pre = row(norm_mix_pre[l])

        lv = da_lambda[l].astype(F32)
        lam = (jnp.exp(jnp.sum(lv[0] * lv[1])) - jnp.exp(jnp.sum(lv[2] * lv[3])) + lam_init).reshape(1, 1)
        da_fin = (lam, da_subln[l].reshape(-1, 1).astype(F32), lam_init)

        dtb = jnp.pad(ssd_dt_bias[l].reshape(1, -1), ((0, 0), (0, LANES - 2 * SSD_HEADS))).astype(F32)
        a_row = jnp.pad(-jnp.exp(ssd_a_log[l].astype(F32)).reshape(1, -1), ((0, 0), (0, LANES - 2 * SSD_HEADS)))
        d_row = row(jnp.repeat(ssd_d[l], SSD_HEAD_DIM))
        g_ssd = row(ssd_norm[l])

        lp = {'hy_w1': hy_w1[l], 'hy_b1': hy_b1[l], 'hy_freq1': hy_freq1[l], 'hy_w2': hy_w2[l],
              'hy_b2': hy_b2[l], 'hy_freq2': hy_freq2[l], 'hy_w3': hy_w3[l]}
        hy_bias_l = hy_bias[l].astype(F32)

        need_ctx = not last
        w_proj = (w_da_qk, w_da_vt, w_ssd, w_mla, row(mla_q_norm[l]), wuq, row(mla_kv_norm[l]), wuk, wuvt, w_hy)
        pl_ = _inproj(x_lat, g_pre, m_lat[0], m_lat[1], w_proj, rope_tabs)
        pc_ = _inproj(x_ctx, g_pre, m_ctx[0], m_ctx[1], w_proj, None)

        heads_first = lambda k, h: k.reshape(bsz, k.shape[1], h, -1).transpose(0, 2, 1, 3)
        kc_da, kc_m = heads_first(pc_['k_da'], DA_HEADS), heads_first(pc_['k_m'], MLA_HEADS)
        k_da = jnp.concatenate([heads_first(pl_['k_da'], DA_HEADS), kc_da], axis=2)
        k_m = jnp.concatenate([heads_first(pl_['k_m'], MLA_HEADS), kc_m], axis=2)
        vt_da = jnp.concatenate([pl_['v_da'], pc_['v_da']], axis=2)
        vt_m = jnp.concatenate([pl_['v_m'], pc_['v_m']], axis=2)
        da_lat = _flash(pl_['q_da'], k_da, vt_da, ncomp=2, heads=DA_HEADS, dq=LANES, da=da_fin)
        mla_lat = _flash(pl_['q_m'], k_m, vt_m, ncomp=1, heads=MLA_HEADS, dq=256)
        cw, cb_ = ssd_conv_w[l].astype(F32), row(ssd_conv_b[l])
        xc_c = _dwconv(pc_['xbc'], cw, cb_, 1, True)
        xc_l = _dwconv(pl_['xbc'], cw, cb_, 1, True)
        zeros = jnp.zeros((bsz, 4, 128, 128), F32)
        yf_c, sf = _ssd_scan(xc_c, pc_['dt'], zeros, dtb, a_row, False)
        yf_l, _ = _ssd_scan(xc_l, pl_['dt'], sf, dtb, a_row, False)
        ssd_ctx, sb = _ssd_scan(xc_c, pc_['dt'], zeros, dtb, a_row, True, fin=(yf_c, pc_['z'], d_row, g_ssd))
        ssd_lat, _ = _ssd_scan(xc_l, pl_['dt'], sb, dtb, a_row, True, fin=(yf_l, pl_['z'], d_row, g_ssd))
        hw, hb = hy_conv_w[l].astype(F32), row(hy_conv_b[l])
        u_l = _dwconv(pl_['hy'], hw, hb, 3, False)
        hid, t_unit, w3f, deltas = _filter_features(n_lat, lp)
        g_l, g_ss = _hy_filter(hid, t_unit, w3f.transpose(1, 2, 0, 3).astype(BF16), deltas.reshape(1, -1), n_lat)
        spec_l = _hy_spectrum(g_l, g_ss, dft_tabs)
        u5 = u_l.reshape(3, bsz, n_lat // DFT_B, DFT_B, HY_WIDTH)
        z1 = _hy_long_conv(u5, 0, u5, 1, spec_l[0], hy_bias_l[0:1], dft_tabs)
        hy_lat = _hy_long_conv(z1[None], 0, u5, 2, spec_l[1], hy_bias_l[1:2], dft_tabs).reshape(bsz, n_lat, HY_WIDTH)

        wbr = jnp.stack([w_br_da[l], w_br_ssd[l], w_br_mla[l], w_br_hy[l]]).astype(BF16)
        wout = w_out[l].astype(BF16)
        g_post = row(norm_mix_post[l])
        w1, w3, w2 = ffn_w1[l].astype(BF16), ffn_w3[l].astype(BF16), ffn_w2[l].astype(BF16)
        gf_pre, gf_post = row(norm_ffn_pre[l]), row(norm_ffn_post[l])

        x_lat = _merge(x_lat, g_pre, m_lat[0], m_lat[1], (da_lat, ssd_lat, mla_lat, hy_lat), w_gate, wbr, wout,
                       g_post, m_lat[2])
        x_lat = _ffn(x_lat, gf_pre, m_lat[3], m_lat[4], w1, w3, w2, gf_post, m_lat[5])

        if need_ctx:
            da_ctx = _flash(pc_['q_da'], kc_da, pc_['v_da'], ncomp=2, heads=DA_HEADS, dq=LANES, da=da_fin)
            mla_ctx = _flash(pc_['q_m'], kc_m, pc_['v_m'], ncomp=1, heads=MLA_HEADS, dq=256)
            u_c = _dwconv(pc_['hy'], hw, hb, 3, False)
            g_c = _hyena_filter(n_ctx, lp).transpose(2, 1, 0, 3).reshape(2 * n_ctx, HY_ORDER * HY_WIDTH)
            spec_c = _ctx_spectrum(g_c, ctx_tabs['fw_full'])
            hy_ctx = _hy_ctx(u_c, spec_c, ctx_tabs['fw'], ctx_tabs['iv'], hy_bias_l)
            x_ctx = _merge(x_ctx, g_pre, m_ctx[0], m_ctx[1], (da_ctx, ssd_ctx, mla_ctx, hy_ctx), w_gate, wbr, wout,
                           g_post, m_ctx[2])
            x_ctx = _ffn(x_ctx, gf_pre, m_ctx[3], m_ctx[4], w1, w3, w2, gf_post, m_ctx[5])
    return x_lat
```

```python
import functools
import math

import jax
import jax.numpy as jnp
import numpy as np
from jax import lax
from jax.experimental import pallas as pl
from jax.experimental.pallas import tpu as pltpu

F32 = jnp.float32
BF16 = jnp.bfloat16
HIGHEST = lax.Precision.HIGHEST

D_MODEL = 1024
DEPTH = 2
GRID_W = 64
EPS = 1e-6
ROPE_BASE = 10000.0
ROT_DIM = 64

DA_HEADS = 4
DA_HEAD_DIM = 64
DA_V_DIM = 128
DA_QK_COLS = 512
DA_WIDTH = 512
COLS_DA = 1536

SSD_HEADS = 8
SSD_HEAD_DIM = 64
SSD_INNER = 512
SSD_GROUPS = 2
SSD_STATE = 128
SSD_CONV = 5
SSD_CHUNK = 128
SSD_XBC = 1024
COLS_SSD = 1552

MLA_HEADS = 4
MLA_Q_LORA = 384
MLA_KV_LORA = 256
MLA_NOPE = 128
MLA_ROPE = 64
MLA_V = 128
MLA_WIDTH = 512
COLS_MLA = 704

HY_WIDTH = 512
HY_ORDER = 2
HY_SHORT = 3
HY_EMB = 33
HY_TARGET = 1e-2
HY_DECAY_SHORT = 0.3
HY_DECAY_LONG = 1.5
COLS_HY = 1536

N_BRANCH = 4
COLS_GATE = 4096
FFN_HIDDEN = 2816

LANES = 128
SUBLANES = 8
VMEM_LIMIT_BYTES = 48 * 2 ** 20
NEG_BIG = -1e30
LOG2_E = 1.4426950408889634

ROW_TILE = 512
ATTN_TQ = 1024
ATTN_TK_MAX = 2816
ATTN_SUM_MAX = 2.0 ** 64
FFN_TH = 768
DFT_B = 128


def _params(sem):
    return pltpu.CompilerParams(dimension_semantics=sem, vmem_limit_bytes=VMEM_LIMIT_BYTES)


def _full(shape):
    nd = len(shape)
    return pl.BlockSpec(shape, lambda *_: (0,) * nd)


def _resident(shape):
    nd = len(shape)
    return pl.BlockSpec(shape, lambda *_: (0,) * nd, pipeline_mode=pl.Buffered(1))


def _dot(a, b):
    return jnp.dot(a, b, preferred_element_type=F32)


def _dot_nt(a, b):
    return lax.dot_general(a, b, (((1,), (1,)), ((), ())), preferred_element_type=F32)


def _dot_tn(a, b):
    return lax.dot_general(a, b, (((0,), (0,)), ((), ())), preferred_element_type=F32)


def _dot_hi(a, b):
    return jnp.dot(a, b, precision=HIGHEST, preferred_element_type=F32)


def _rms(x, g):
    return x * lax.rsqrt(jnp.mean(x * x, axis=-1, keepdims=True) + EPS) * g


def _norm_mod(x, g, shift, scale):
    return _rms(x, g) * (1.0 + scale) + shift


def _silu(x):
    return x * jax.nn.sigmoid(x)


def _rope128(x, cosf, sins, first_half):
    partner = jnp.where(first_half, pltpu.roll(x, 96, 1), pltpu.roll(x, 32, 1))
    return x * cosf + partner * sins


def _first_half_mask(rows):
    return (lax.broadcasted_iota(jnp.int32, (rows, LANES), 1) & 32) == 0


def _mod_kernel(c_ref, w_ref, b_ref, o_ref):
    act = _silu(c_ref[...]).astype(BF16)
    o_ref[...] = _dot(act, w_ref[...]) + b_ref[...]


def _modulation(c_rows, w, b):
    n = w.shape[1] // D_MODEL
    return pl.pallas_call(
        _mod_kernel,
        grid=(n,),
        in_specs=[_full((SUBLANES, D_MODEL)),
                  pl.BlockSpec((D_MODEL, D_MODEL), lambda j: (0, j)),
                  pl.BlockSpec((1, D_MODEL), lambda j: (0, j))],
        out_specs=pl.BlockSpec((SUBLANES, D_MODEL), lambda j: (0, j)),
        out_shape=jax.ShapeDtypeStruct((SUBLANES, w.shape[1]), F32),
        compiler_params=_params(("parallel",)),
        name="modulation",
    )(c_rows, w, b)


INPROJ_TILE = 512


def _inproj_da_part(h, rope_cs, w_ref, wvt_ref, q_ref, k_ref, vt_ref):
    first = _first_half_mask(h.shape[0])
    scale = DA_HEAD_DIM ** -0.5 * LOG2_E
    for out_ref, c0, is_q in ((q_ref, 0, True), (k_ref, 512, False)):
        res = _dot(h, w_ref[:, c0:c0 + 512])
        for i in range(4):
            t = res[:, i * LANES:(i + 1) * LANES]
            if rope_cs is not None:
                t = _rope128(t, rope_cs[0], rope_cs[1], first)
            if is_q:
                q_ref[0, :, i * LANES:(i + 1) * LANES] = (t * scale).astype(BF16)
            else:
                k_ref[0, i] = t.astype(BF16)
    vt_ref[0] = _dot_nt(wvt_ref[...], h).astype(BF16)


def _inproj_mla_part(h, rope_cs, w_ref, gq_ref, wuq_ref, gkv_ref, wuk_ref, wuvt_ref, q_ref, k_ref, vt_ref):
    rope = rope_cs is not None
    first = _first_half_mask(h.shape[0])
    res = _dot(h, w_ref[...])
    cq = _rms(res[:, :MLA_Q_LORA], gq_ref[...]).astype(BF16)
    ckv = _rms(res[:, MLA_Q_LORA:MLA_Q_LORA + MLA_KV_LORA], gkv_ref[...]).astype(BF16)
    kr = res[:, MLA_Q_LORA + MLA_KV_LORA:]
    q = _dot(cq, wuq_ref[...])
    kn = _dot(ckv, wuk_ref[...])
    if rope:
        kr = _rope128(kr, rope_cs[0], rope_cs[1], first)
    kr = kr.astype(BF16)
    scale = (MLA_NOPE + MLA_ROPE) ** -0.5 * LOG2_E
    for hh in range(MLA_HEADS):
        c0 = hh * 256
        qr = q[:, c0 + LANES:c0 + 256]
        if rope:
            qr = _rope128(qr, rope_cs[0], rope_cs[1], first)
        q_ref[0, :, c0:c0 + LANES] = (q[:, c0:c0 + LANES] * scale).astype(BF16)
        q_ref[0, :, c0 + LANES:c0 + 256] = (qr * scale).astype(BF16)
        k_ref[0, hh, :, 0:LANES] = kn[:, hh * LANES:(hh + 1) * LANES].astype(BF16)
        k_ref[0, hh, :, LANES:256] = kr
    vt_ref[0] = _dot_nt(wuvt_ref[...], ckv).astype(BF16)


def _inproj_kernel(*refs, rope):
    x_ref, g_ref, sh_ref, sc_ref = refs[:4]
    (w_da, w_da_vt, w_ssd, w_mla, gq, wuq, gkv, wuk, wuvt, w_hy) = refs[4:14]
    pos = 14
    rope_cs = None
    if rope:
        rope_cs = (refs[14][...], refs[15][...])
        pos = 16
    q_da, k_da, z_ref, xbc_ref, dt_ref, q_m, k_m, hy_ref, vt_da, vt_m = refs[pos:pos + 10]
    h = _norm_mod(x_ref[0], g_ref[...], sh_ref[0], sc_ref[0]).astype(BF16)
    _inproj_da_part(h, rope_cs, w_da, w_da_vt, q_da, k_da, vt_da)
    z_ref[0] = _dot(h, w_ssd[:, 0:512])
    xbc_ref[0] = _dot(h, w_ssd[:, 512:1536])
    dt_ref[0] = _dot(h, w_ssd[:, 1536:1664])
    _inproj_mla_part(h, rope_cs, w_mla, gq, wuq, gkv, wuk, wuvt, q_m, k_m, vt_m)
    for c0 in range(0, COLS_HY, 512):
        hy_ref[0, :, c0:c0 + 512] = _dot(h, w_hy[:, c0:c0 + 512])


def _inproj(x, g, shift, scale, weights, rope_tabs):
    b, l, _ = x.shape
    tm = min(INPROJ_TILE, l)
    rope = rope_tabs is not None
    vec = pl.BlockSpec((1, 1, D_MODEL), lambda bi, i: (bi, 0, 0))
    tok_last = pl.BlockSpec((1, 512, tm), lambda bi, i: (bi, 0, i))
    names = ('q_da', 'k_da', 'z', 'xbc', 'dt', 'q_m', 'k_m', 'hy')
    dims = (512, (DA_HEADS, LANES), 512, 1024, LANES, 1024, (MLA_HEADS, 256), COLS_HY)
    dtypes = (BF16, BF16, F32, F32, F32, BF16, BF16, F32)

    def row(d):
        if isinstance(d, tuple):
            return pl.BlockSpec((1, d[0], tm, d[1]), lambda bi, i: (bi, 0, i, 0))
        return pl.BlockSpec((1, tm, d), lambda bi, i: (bi, i, 0))

    shape = lambda d: (b, d[0], l, d[1]) if isinstance(d, tuple) else (b, l, d)
    in_specs = [row(D_MODEL), _full((1, D_MODEL)), vec, vec] + [_resident(w.shape) for w in weights]
    args = [x, g, shift, scale, *weights]
    if rope:
        in_specs += [pl.BlockSpec((tm, LANES), lambda bi, i: (i, 0))] * 2
        args += list(rope_tabs)
    outs = pl.pallas_call(
        functools.partial(_inproj_kernel, rope=rope),
        grid=(b, l // tm),
        in_specs=in_specs,
        out_specs=[row(d) for d in dims] + [tok_last, tok_last],
        out_shape=[jax.ShapeDtypeStruct(shape(d), dt) for d, dt in zip(dims, dtypes)]
                  + [jax.ShapeDtypeStruct((b, 512, l), BF16)] * 2,
        compiler_params=_params(("parallel", "parallel")),
        name="inproj",
    )(*args)
    return dict(zip(names + ('v_da', 'v_m'), outs))


def _flash_kernel(*refs, ncomp, has_lat, da_finish, lam_init):
    q_ref, kc_ref, vtc_ref = refs[:3]
    pos = 3
    if has_lat:
        k_ref, vt_ref = refs[3:5]
        pos = 5
    if da_finish:
        lam_ref, sub_ref = refs[pos:pos + 2]
        pos += 2
    o_ref, m_ref, l_ref, acc_ref = refs[pos:pos + 4]
    j = pl.program_id(3)
    nk = pl.num_programs(3)
    q = q_ref[0]
    tq = q.shape[0]
    if ncomp == 2:
        lo = lax.broadcasted_iota(jnp.int32, (tq, LANES), 1) < DA_HEAD_DIM
        zero = jnp.zeros_like(q)
        qs = (jnp.where(lo, q, zero), jnp.where(lo, zero, q))
    else:
        qs = (q,)

    @pl.when(j == 0)
    def _():
        kc, vtc = kc_ref[0, 0], vtc_ref[0]
        for c in range(ncomp):
            st = _dot_nt(kc, qs[c])
            m0 = jnp.max(st, axis=0, keepdims=True)
            p0 = jnp.exp2(st - m0)
            m_ref[c] = m0
            l_ref[c] = jnp.sum(p0, axis=0, keepdims=True)
            acc_ref[c] = _dot(vtc, p0.astype(BF16))

    k = k_ref[0, 0] if has_lat else None
    vt = vt_ref[0] if has_lat else None
    for c in range(ncomp if has_lat else 0):
        m = m_ref[c]
        p = jnp.exp2(_dot_nt(k, qs[c]) - m)
        l_add = jnp.sum(p, axis=0, keepdims=True)
        pv = _dot(vt, p.astype(BF16))
        safe = jnp.max(l_add) <= ATTN_SUM_MAX

        @pl.when(safe)
        def _():
            l_ref[c] += l_add
            acc_ref[c] += pv

        @pl.when(jnp.logical_not(safe))
        def _():
            st = _dot_nt(k, qs[c])
            m_new = jnp.maximum(m, jnp.max(st, axis=0, keepdims=True))
            alpha = jnp.exp2(m - m_new)
            p2 = jnp.exp2(st - m_new)
            l_ref[c] = alpha * l_ref[c] + jnp.sum(p2, axis=0, keepdims=True)
            acc_ref[c] = alpha * acc_ref[c] + _dot(vt, p2.astype(BF16))
            m_ref[c] = m_new

    @pl.when(j == nk - 1)
    def _():
        o = acc_ref[0] / l_ref[0]
        if da_finish:
            o = o - lam_ref[...] * (acc_ref[1] / l_ref[1])
            ms = jnp.mean(o * o, axis=0, keepdims=True)
            o = o * lax.rsqrt(ms + EPS) * sub_ref[...] * (1.0 - lam_init)
        o_ref[0] = o.T.astype(o_ref.dtype)


def _key_tile(lk):
    best = LANES
    for t in range(LANES, min(lk, ATTN_TK_MAX) + 1, LANES):
        if lk % t == 0:
            best = t
    return best


def _flash(q, k_ctx, vt_ctx, k_lat, vt_lat, *, ncomp, heads, dq, da=None):
    b, lq, _ = q.shape
    lc = k_ctx.shape[2]
    has_lat = k_lat is not None
    tq = min(ATTN_TQ, lq)
    nq = lq // tq
    in_specs = [pl.BlockSpec((1, tq, dq), lambda bi, h, i, j: (bi, i, h)),
                pl.BlockSpec((1, 1, lc, dq), lambda bi, h, i, j: (bi, h, 0, 0)),
                pl.BlockSpec((1, LANES, lc), lambda bi, h, i, j: (bi, h, 0))]
    args = [q, k_ctx, vt_ctx]
    nk = 1
    if has_lat:
        tk = _key_tile(k_lat.shape[2])
        nk = k_lat.shape[2] // tk
        in_specs += [pl.BlockSpec((1, 1, tk, dq), lambda bi, h, i, j: (bi, h, j, 0)),
                     pl.BlockSpec((1, LANES, tk), lambda bi, h, i, j: (bi, h, j))]
        args += [k_lat, vt_lat]
    lam_init = 0.0
    if da is not None:
        lam, subln, lam_init = da
        in_specs += [_full((1, 1)), _full((LANES, 1))]
        args += [lam, subln]
    kernel = functools.partial(_flash_kernel, ncomp=ncomp, has_lat=has_lat, da_finish=da is not None,
                               lam_init=lam_init)
    return pl.pallas_call(
        kernel,
        grid=(b, heads, nq, nk),
        in_specs=in_specs,
        out_specs=pl.BlockSpec((1, tq, LANES), lambda bi, h, i, j: (bi, i, h)),
        out_shape=jax.ShapeDtypeStruct((b, lq, heads * LANES), BF16),
        scratch_shapes=[pltpu.VMEM((ncomp, 1, tq), F32), pltpu.VMEM((ncomp, 1, tq), F32),
                        pltpu.VMEM((ncomp, LANES, tq), F32)],
        compiler_params=_params(("parallel", "parallel", "parallel", "arbitrary")),
        name="flash_da" if ncomp == 2 else "flash_mla",
    )(*args)


def _dwconv_kernel(x_ref, w_ref, b_ref, o_ref, pad_ref, *, taps, act, chunk):
    l = x_ref.shape[1]
    pad_ref[0:SUBLANES, :] = jnp.zeros((SUBLANES, LANES), F32)
    pad_ref[SUBLANES + l:2 * SUBLANES + l, :] = jnp.zeros((SUBLANES, LANES), F32)
    pad_ref[SUBLANES:SUBLANES + l, :] = x_ref[0]
    w = w_ref[...]
    bias = b_ref[...]

    def body(i, carry):
        base = pl.multiple_of(i * chunk, chunk)
        acc = jnp.zeros((chunk, LANES), F32) + bias
        for t in range(taps):
            acc = acc + w[t:t + 1, :] * pad_ref[pl.ds(base + SUBLANES + t - taps // 2, chunk), :]
        if act:
            acc = _silu(acc)
        o_ref[0, 0, pl.ds(base, chunk), :] = acc
        return carry

    lax.fori_loop(0, l // chunk, body, 0)


def _dwconv(x, w, bias, groups, act):
    b, l, c = x.shape
    taps = w.shape[0]
    cg = c // groups
    per = cg // LANES
    chunk = min(512, l)
    return pl.pallas_call(
        functools.partial(_dwconv_kernel, taps=taps, act=act, chunk=chunk),
        grid=(b, c // LANES),
        in_specs=[pl.BlockSpec((1, l, LANES), lambda bi, ct: (bi, 0, ct)),
                  pl.BlockSpec((taps, LANES), lambda bi, ct: (0, ct)),
                  pl.BlockSpec((1, LANES), lambda bi, ct: (0, ct))],
        out_specs=pl.BlockSpec((1, 1, l, LANES), lambda bi, ct: (ct // per, bi, 0, ct % per)),
        out_shape=jax.ShapeDtypeStruct((groups, b, l, cg), F32),
        scratch_shapes=[pltpu.VMEM((l + 2 * SUBLANES, LANES), F32)],
        compiler_params=_params(("parallel", "parallel")),
        name="dwconv",
    )(x, w, bias)


SSD_STEP_CHUNKS = 4

def _ssd_kernel(*refs, reverse, finish):
    if finish:
        (xc_ref, dt_ref, init_ref, dtb_ref, a_ref, yo_ref, z_ref, d_ref, g_ref,
         y_ref, fin_ref, st_ref) = refs
    else:
        xc_ref, dt_ref, init_ref, dtb_ref, a_ref, y_ref, fin_ref, st_ref = refs
    j = pl.program_id(1)
    nc = pl.num_programs(1)
    q = SSD_CHUNK

    @pl.when(j == 0)
    def _():
        st_ref[...] = init_ref[0]

    r = lax.broadcasted_iota(jnp.int32, (q, q), 0)
    c = lax.broadcasted_iota(jnp.int32, (q, q), 1)
    tri = (c >= r) if reverse else (c <= r)
    off = SSD_HEADS if reverse else 0
    lane_lo = c < SSD_HEAD_DIM
    row_lo = r < SSD_HEAD_DIM
    n_sub = xc_ref.shape[2] // q
    for sub in (range(n_sub - 1, -1, -1) if reverse else range(n_sub)):
        rows = slice(sub * q, (sub + 1) * q)
        _ssd_chunk(xc_ref[0, 0, rows, :], dt_ref[0, rows, :], dtb_ref, a_ref, st_ref, y_ref, rows, tri, off,
                   lane_lo, row_lo, reverse,
                   (yo_ref[0, rows, :], z_ref[0, rows, :], d_ref, g_ref) if finish else None)

    @pl.when(j == nc - 1)
    def _():
        fin_ref[0] = st_ref[...]


def _ssd_chunk(xc, dt_raw, dtb_ref, a_ref, st_ref, y_ref, rows, tri, off, lane_lo, row_lo, reverse, fin):
    q = SSD_CHUNK
    finish = fin is not None
    pre = dt_raw + dtb_ref[...]
    dt = jnp.maximum(pre, 0.0) + jnp.log1p(jnp.exp(-jnp.abs(pre)))
    a = dt * a_ref[...]
    cum = _dot_hi(tri.astype(F32), a)
    cum_t = cum.T
    tot = cum[0:1, :] if reverse else cum[q - 1:q, :]
    ys = []
    for g in range(SSD_GROUPS):
        bm = xc[:, 512 + g * 128:512 + (g + 1) * 128].astype(BF16)
        cm = xc[:, 768 + g * 128:768 + (g + 1) * 128].astype(BF16)
        cb = _dot_nt(cm, bm)
        for pp in range(2):
            hp = 2 * g + pp
            l0 = off + 2 * hp
            l1 = l0 + 1
            col0, col1 = cum[:, l0:l0 + 1], cum[:, l1:l1 + 1]
            dec0 = jnp.exp(jnp.where(tri, col0 - cum_t[l0:l0 + 1, :], NEG_BIG))
            dec1 = jnp.exp(jnp.where(tri, col1 - cum_t[l1:l1 + 1, :], NEG_BIG))
            xdt = xc[:, hp * 128:(hp + 1) * 128] * jnp.where(lane_lo, dt[:, l0:l0 + 1], dt[:, l1:l1 + 1])
            xdt_b = xdt.astype(BF16)
            zero = jnp.zeros_like(xdt_b)
            y = (_dot((cb * dec0).astype(BF16), jnp.where(lane_lo, xdt_b, zero))
                 + _dot((cb * dec1).astype(BF16), jnp.where(lane_lo, zero, xdt_b)))
            colp = jnp.where(lane_lo, col0, col1)
            totp = jnp.where(lane_lo, tot[:, l0:l0 + 1], tot[:, l1:l1 + 1])
            contrib = _dot_tn((xdt * jnp.exp(totp - colp)).astype(BF16), bm)
            prev = st_ref[hp]
            y = y + _dot_nt(cm, prev.astype(BF16)) * jnp.exp(colp)
            st_ref[hp] = prev * jnp.exp(jnp.where(row_lo, tot[:, l0:l0 + 1], tot[:, l1:l1 + 1])) + contrib
            if finish:
                ys.append(y)
            else:
                y_ref[0, rows, hp * 128:(hp + 1) * 128] = y

    if finish:
        yo, z, d_ref, g_ref = fin
        yt = jnp.concatenate(ys, axis=1) + yo + d_ref[...] * xc[:, :SSD_INNER]
        y_ref[0, rows, :] = _rms(yt * _silu(z), g_ref[...]).astype(y_ref.dtype)


def _ssd_scan(xc, dt_raw, init, dtb, a_row, reverse, fin=None):
    _, b, l, _ = xc.shape
    step = SSD_CHUNK * min(SSD_STEP_CHUNKS, l // SSD_CHUNK)
    nc = l // step
    cidx = (lambda j: nc - 1 - j) if reverse else (lambda j: j)
    rows = lambda d: pl.BlockSpec((1, step, d), lambda bi, j: (bi, cidx(j), 0))
    st_spec = pl.BlockSpec((1, 4, 128, 128), lambda bi, j: (bi, 0, 0, 0))
    in_specs = [pl.BlockSpec((1, 1, step, SSD_XBC), lambda bi, j: (0, bi, cidx(j), 0)),
                rows(LANES), st_spec, _full((1, LANES)), _full((1, LANES))]
    args = [xc, dt_raw, init, dtb, a_row]
    finish = fin is not None
    if finish:
        y_other, z, d_row, g = fin
        in_specs += [rows(SSD_INNER), rows(SSD_INNER), _full((1, SSD_INNER)), _full((1, SSD_INNER))]
        args += [y_other, z, d_row, g]
    return pl.pallas_call(
        functools.partial(_ssd_kernel, reverse=reverse, finish=finish),
        grid=(b, nc),
        in_specs=in_specs,
        out_specs=[rows(SSD_INNER), st_spec],
        out_shape=[jax.ShapeDtypeStruct((b, l, SSD_INNER), BF16 if finish else F32),
                   jax.ShapeDtypeStruct((b, 4, 128, 128), F32)],
        scratch_shapes=[pltpu.VMEM((4, 128, 128), F32)],
        compiler_params=_params(("parallel", "arbitrary")),
        name="ssd_bwd" if reverse else "ssd_fwd",
    )(*args)


def _split_bf16(x):
    hi = x.astype(BF16)
    return hi, (x - hi.astype(F32)).astype(BF16)


def _dot_split(a3_ref, b):
    b_hi, b_lo = _split_bf16(b)
    return _dot(a3_ref[...], jnp.concatenate([b_hi, b_hi, b_lo], axis=0))


def _lane_tile(t, width):
    return jnp.concatenate([t] * (width // LANES), axis=1)


def _hy_dft_a_kernel(z_ref, fa_ref, tw_ref, o_ref, *, pair):
    a8, c = o_ref.shape[2], o_ref.shape[5]
    for i in range(SUBLANES):
        if pair:
            z = jnp.concatenate([z_ref[0, 0, :, i, :], z_ref[0, 1, :, i, :]], axis=0)
        else:
            z = z_ref[0, 0, i]
        res = _dot_split(fa_ref, z)
        half = res.shape[0] // 2
        xr, xi = res[:half], res[half:]
        tr, ti = _lane_tile(tw_ref[i, 0], c), _lane_tile(tw_ref[i, 1], c)
        o_ref[0, 0, :, i] = (xr * tr - xi * ti).reshape(a8, SUBLANES, c)
        o_ref[0, 1, :, i] = (xr * ti + xi * tr).reshape(a8, SUBLANES, c)


def _hy_dft_a(z5, group, fa3, tw, pair):
    s, c = z5.shape[1], z5.shape[4]
    dft_a = tw.shape[2]
    if pair:
        p = s // 2
        z_spec = pl.BlockSpec((1, 2, z5.shape[2], SUBLANES, c), lambda pi, jb: (group, pi, 0, jb, 0))
    else:
        p = s
        z_spec = pl.BlockSpec((1, 1, SUBLANES, z5.shape[3], c), lambda pi, jb: (group, pi, jb, 0, 0))
    return pl.pallas_call(
        functools.partial(_hy_dft_a_kernel, pair=pair),
        grid=(p, DFT_B // SUBLANES),
        in_specs=[z_spec,
                  _full(fa3.shape),
                  pl.BlockSpec((SUBLANES, 2, dft_a, LANES), lambda pi, jb: (jb, 0, 0, 0))],
        out_specs=pl.BlockSpec((1, 2, dft_a // SUBLANES, SUBLANES, SUBLANES, c),
                               lambda pi, jb: (pi, 0, 0, jb, 0, 0)),
        out_shape=jax.ShapeDtypeStruct((p, 2, dft_a // SUBLANES, DFT_B, SUBLANES, c), F32),
        compiler_params=_params(("parallel", "parallel")),
        name="hyena_dft_a",
    )(z5, fa3, tw)


def _hy_dft_b_kernel(*refs, conv):
    if conv:
        y_ref, s_ref, mf_ref, mi_ref, o_ref = refs
    else:
        y_ref, mf_ref, scale_ref, o_ref = refs
    for kl in range(SUBLANES):
        y = jnp.concatenate([y_ref[0, 0, 0, :, kl, :], y_ref[0, 1, 0, :, kl, :]], axis=0)
        out = _dot_split(mf_ref, y)
        if conv:
            xr, xi = out[:DFT_B], out[DFT_B:]
            sr, si = s_ref[0, 0, :, kl, :], s_ref[1, 0, :, kl, :]
            out = _dot_split(mi_ref, jnp.concatenate([xr * sr - xi * si, xr * si + xi * sr], axis=0))
        else:
            out = out * scale_ref[0]
        o_ref[0, 0, 0, :, kl, :] = out[:DFT_B]
        o_ref[0, 1, 0, :, kl, :] = out[DFT_B:]


def _hy_dft_b(y1, mf3, spec_mi=None, scale=None):
    p, _, a8, _, _, c = y1.shape
    conv = spec_mi is not None
    blk = pl.BlockSpec((1, 2, 1, DFT_B, SUBLANES, c), lambda kh, pi: (pi, 0, kh, 0, 0, 0))
    in_specs, args = [blk], [y1]
    if conv:
        in_specs += [pl.BlockSpec((2, 1, DFT_B, SUBLANES, c), lambda kh, pi: (0, kh, 0, 0, 0)), _full(mf3.shape),
                     _full(spec_mi[1].shape)]
        args += [spec_mi[0], mf3, spec_mi[1]]
    else:
        in_specs += [_full(mf3.shape), pl.BlockSpec((1, 1, c), lambda kh, pi: (pi, 0, 0))]
        args += [mf3, scale]
    return pl.pallas_call(
        functools.partial(_hy_dft_b_kernel, conv=conv),
        grid=(a8, p),
        in_specs=in_specs,
        out_specs=blk,
        out_shape=jax.ShapeDtypeStruct(y1.shape, F32),
        compiler_params=_params(("parallel", "parallel")),
        name="hyena_dft_b" if conv else "hyena_spectrum_b",
    )(*args)


def _hy_dft_c_kernel(y_ref, tw_ref, fc_ref, gate_ref, zp_ref, bias_ref, o_ref):
    a8, c = y_ref.shape[2], y_ref.shape[5]
    na = o_ref.shape[1]
    bias = bias_ref[...]
    for i in range(SUBLANES):
        yr = y_ref[0, 0, :, i].reshape(a8 * SUBLANES, c)
        yi = y_ref[0, 1, :, i].reshape(a8 * SUBLANES, c)
        tr, ti = _lane_tile(tw_ref[i, 0], c), _lane_tile(tw_ref[i, 1], c)
        conv = _dot_split(fc_ref, jnp.concatenate([yr * tr + yi * ti, yi * tr - yr * ti], axis=0))
        for s in range(2):
            o_ref[s, :, i, :] = gate_ref[0, s, :, i, :] * (conv[s * na:(s + 1) * na] + bias * zp_ref[0, s, :, i, :])


def _hy_filter_kernel(hf_ref, hb_ref, tf_ref, tb_ref, w_ref, d_ref, g_ref, ss_ref):
    j = pl.program_id(1)
    rows, width = hf_ref.shape[0], g_ref.shape[3]
    na = rows // SUBLANES
    fwd = _dot(hf_ref[...].astype(BF16), w_ref[0, 0]) * jnp.exp(-tf_ref[...] * d_ref[...])
    bwd = _dot(hb_ref[...].astype(BF16), w_ref[1, 0]) * jnp.exp(-tb_ref[...] * d_ref[...])
    first = jnp.logical_and(lax.broadcasted_iota(jnp.int32, (rows, width), 0) == 0, j == 0)
    fwd = jnp.where(first, fwd + bwd, fwd)
    bwd = jnp.where(first, 0.0, bwd)
    g_ref[0, :, :na] = fwd.reshape(SUBLANES, na, width)
    g_ref[0, :, na:] = bwd.reshape(SUBLANES, na, width)
    ss = jnp.sum(fwd * fwd, axis=0, keepdims=True) + jnp.sum(bwd * bwd, axis=0, keepdims=True)

    @pl.when(j == 0)
    def _():
        ss_ref[0] = ss

    @pl.when(j > 0)
    def _():
        ss_ref[0] += ss


def _hy_filter(hid, t_unit, w3, deltas, n):
    na = n // DFT_B
    rows = SUBLANES * na
    order, hidden, width = w3.shape[1], w3.shape[2], w3.shape[3]
    lag_f = (DFT_B * np.arange(na)[None, :] + np.arange(DFT_B)[:, None]).reshape(-1)
    lag_b = n - lag_f
    lag_b[0] = 0
    hf, hb = hid[lag_f], hid[lag_b]
    tf, tb = t_unit[lag_f][:, None], t_unit[lag_b][:, None]
    rspec = lambda d: pl.BlockSpec((rows, d), lambda o, j: (j, 0))
    return pl.pallas_call(
        _hy_filter_kernel,
        grid=(order, DFT_B // SUBLANES),
        in_specs=[rspec(hidden), rspec(hidden), rspec(1), rspec(1),
                  pl.BlockSpec((2, 1, hidden, width), lambda o, j: (0, o, 0, 0)), _full((1, width))],
        out_specs=[pl.BlockSpec((1, SUBLANES, 2 * na, width), lambda o, j: (o, j, 0, 0)),
                   pl.BlockSpec((1, 1, width), lambda o, j: (o, 0, 0))],
        out_shape=[jax.ShapeDtypeStruct((order, DFT_B, 2 * na, width), F32),
                   jax.ShapeDtypeStruct((order, 1, width), F32)],
        compiler_params=_params(("parallel", "arbitrary")),
        name="hyena_filter",
    )(hf, hb, tf, tb, w3, deltas)


def _hy_spectrum(g, sumsq, tabs):
    y1 = _hy_dft_a(g[None], 0, tabs['fa_full'], tabs['tw'], False)
    return _hy_dft_b(y1, tabs['mf'], scale=lax.rsqrt(sumsq + EPS))


def _hy_long_conv(z5, zgroup, u5, gate_group, spec_l, bias_row, tabs):
    fc, tw = tabs['fc'], tabs['tw']
    _, b, na, _, c = z5.shape
    dft_a = 2 * na
    y1 = _hy_dft_a(z5, zgroup, tabs['fa'], tw, True)
    y2 = _hy_dft_b(y1, tabs['mf'], (spec_l, tabs['mi']))
    seq = lambda g: pl.BlockSpec((1, 2, na, SUBLANES, c), lambda pi, jb: (g, pi, 0, jb, 0))
    return pl.pallas_call(
        _hy_dft_c_kernel,
        grid=(b // 2, DFT_B // SUBLANES),
        in_specs=[pl.BlockSpec((1, 2, dft_a // SUBLANES, SUBLANES, SUBLANES, c), lambda pi, jb: (pi, 0, 0, jb, 0, 0)),
                  pl.BlockSpec((SUBLANES, 2, dft_a, LANES), lambda pi, jb: (jb, 0, 0, 0)),
                  _full(fc.shape), seq(gate_group), seq(zgroup), _full((1, c))],
        out_specs=pl.BlockSpec((2, na, SUBLANES, c), lambda pi, jb: (pi, 0, jb, 0)),
        out_shape=jax.ShapeDtypeStruct((b, na, DFT_B, c), F32),
        compiler_params=_params(("parallel", "parallel")),
        name="hyena_dft_c",
    )(y2, tw, fc, u5, z5, bias_row)


def _hy_ctx_kernel(v_ref, x1_ref, x2_ref, s_ref, fw_ref, iv_ref, bias_ref, o_ref):
    z = v_ref[0, 0]
    n2 = fw_ref.shape[0] // 2
    for o, gate_ref in enumerate((x1_ref, x2_ref)):
        x = _dot_split(fw_ref, z)
        xr, xi = x[:n2], x[n2:]
        sr, si = s_ref[0, :, o * HY_WIDTH:(o + 1) * HY_WIDTH], s_ref[1, :, o * HY_WIDTH:(o + 1) * HY_WIDTH]
        prod = jnp.concatenate([xr * sr - xi * si, xr * si + xi * sr], axis=0)
        conv = _dot_split(iv_ref, prod)
        z = gate_ref[0, 0] * (conv + bias_ref[o:o + 1, :] * z)
    o_ref[0] = z.astype(o_ref.dtype)


def _ctx_spectrum_kernel(g_ref, f_ref, o_ref):
    res = _dot_split(f_ref, g_ref[...])
    half = res.shape[0] // 2
    o_ref[0] = res[:half]
    o_ref[1] = res[half:]


def _ctx_spectrum(g, f3):
    n2, c = g.shape
    return pl.pallas_call(
        _ctx_spectrum_kernel,
        in_specs=[_full(g.shape), _full(f3.shape)],
        out_specs=_full((2, n2, c)),
        out_shape=jax.ShapeDtypeStruct((2, n2, c), F32),
        compiler_params=pltpu.CompilerParams(vmem_limit_bytes=VMEM_LIMIT_BYTES),
        name="hyena_ctx_spectrum",
    )(g, f3)


def _hy_ctx(u4, spec_c, fw, iv, bias):
    _, b, n, c = u4.shape
    grp = lambda g: pl.BlockSpec((1, 1, n, c), lambda bi: (g, bi, 0, 0))
    return pl.pallas_call(
        _hy_ctx_kernel,
        grid=(b,),
        in_specs=[grp(0), grp(1), grp(2), _full(spec_c.shape), _full(fw.shape), _full(iv.shape), _full(bias.shape)],
        out_specs=pl.BlockSpec((1, n, c), lambda bi: (bi, 0, 0)),
        out_shape=jax.ShapeDtypeStruct((b, n, c), BF16),
        compiler_params=_params(("parallel",)),
        name="hyena_ctx",
    )(u4, u4, u4, spec_c, fw, iv, bias)


def _merge_kernel(x_ref, g_ref, sh_ref, sc_ref, da_ref, ssd_ref, mla_ref, hy_ref, wgate_ref, wbr_ref, wout_ref,
                  gpost_ref, mg_ref, o_ref):
    x = x_ref[0]
    h = _norm_mod(x, g_ref[...], sh_ref[0], sc_ref[0]).astype(BF16)
    mixed = None
    for k, br_ref in enumerate((da_ref, ssd_ref, mla_ref, hy_ref)):
        gate = jax.nn.sigmoid(_dot(h, wgate_ref[:, k * D_MODEL:(k + 1) * D_MODEL]))
        t = gate * _dot(br_ref[0].astype(BF16), wbr_ref[k])
        mixed = t if mixed is None else mixed + t
    y = _dot(mixed.astype(BF16), wout_ref[...])
    o_ref[0] = x + mg_ref[0] * _rms(y, gpost_ref[...])


def _merge(x, g, shift, scale, branches, wgate, wbr, wout, gpost, mgate):
    b, l, _ = x.shape
    tm = min(ROW_TILE, l)
    row = lambda d: pl.BlockSpec((1, tm, d), lambda bi, i: (bi, i, 0))
    vec = pl.BlockSpec((1, 1, D_MODEL), lambda bi, i: (bi, 0, 0))
    return pl.pallas_call(
        _merge_kernel,
        grid=(b, l // tm),
        in_specs=[row(D_MODEL), _full((1, D_MODEL)), vec, vec] + [row(512)] * 4
                 + [_full(wgate.shape), _full(wbr.shape), _full(wout.shape), _full((1, D_MODEL)), vec],
        out_specs=row(D_MODEL),
        out_shape=jax.ShapeDtypeStruct((b, l, D_MODEL), F32),
        compiler_params=_params(("parallel", "parallel")),
        name="merge",
    )(x, g, shift, scale, *branches, wgate, wbr, wout, gpost, mgate)


def _ffn_kernel(x_ref, g_ref, sh_ref, sc_ref, w1_ref, w3_ref, w2_ref, gpost_ref, mg_ref, o_ref):
    x = x_ref[0]
    h = _norm_mod(x, g_ref[...], sh_ref[0], sc_ref[0]).astype(BF16)
    f = None
    for c0 in range(0, FFN_HIDDEN, FFN_TH):
        c1 = min(c0 + FFN_TH, FFN_HIDDEN)
        t = (_silu(_dot(h, w1_ref[:, c0:c1])) * _dot(h, w3_ref[:, c0:c1])).astype(BF16)
        part = _dot(t, w2_ref[c0:c1, :])
        f = part if f is None else f + part
    o_ref[0] = x + mg_ref[0] * _rms(f, gpost_ref[...])


def _ffn(x, g, shift, scale, w1, w3, w2, gpost, mgate):
    b, l, _ = x.shape
    tm = min(ROW_TILE, l)
    row = pl.BlockSpec((1, tm, D_MODEL), lambda bi, i: (bi, i, 0))
    vec = pl.BlockSpec((1, 1, D_MODEL), lambda bi, i: (bi, 0, 0))
    return pl.pallas_call(
        _ffn_kernel,
        grid=(b, l // tm),
        in_specs=[row, _full((1, D_MODEL)), vec, vec, _resident(w1.shape), _resident(w3.shape), _resident(w2.shape),
                  _full((1, D_MODEL)), vec],
        out_specs=row,
        out_shape=jax.ShapeDtypeStruct((b, l, D_MODEL), F32),
        compiler_params=_params(("parallel", "parallel")),
        name="ffn",
    )(x, g, shift, scale, w1, w3, w2, gpost, mgate)


def _rope_tables(n_lat):
    t = jnp.arange(n_lat)
    pos_row = (t // GRID_W).astype(F32)
    pos_col = (t % GRID_W).astype(F32)
    quarter = ROT_DIM // 4
    inv = ROPE_BASE ** (-jnp.arange(quarter, dtype=F32) / quarter)
    ang = jnp.concatenate([pos_row[:, None] * inv, pos_col[:, None] * inv], axis=-1)
    cos, sin = jnp.cos(ang), jnp.sin(ang)
    cosf = jnp.tile(cos, (1, 4))
    sins = jnp.tile(jnp.concatenate([-sin, sin], axis=-1), (1, 2))
    return cosf, sins


def _filter_features(n, lp):
    t = jnp.arange(n, dtype=F32)
    t_unit = t / (n - 1)
    bands = (HY_EMB - 1) // 2
    band_f = jnp.linspace(1e-4, bands - 1, bands, dtype=F32)
    w = 2.0 * math.pi * t / n
    feats = jnp.concatenate([t_unit[:, None], jnp.cos(w[:, None] * band_f), -jnp.sin(w[:, None] * band_f)], axis=-1)
    hid = jnp.sin(lp['hy_freq1'] * (feats @ lp['hy_w1'] + lp['hy_b1']))
    hid = jnp.sin(lp['hy_freq2'] * (hid @ lp['hy_w2'] + lp['hy_b2']))
    w3 = lp['hy_w3'].reshape(-1, 2, HY_ORDER, HY_WIDTH)
    deltas = jnp.abs(jnp.linspace(math.log(HY_TARGET) / HY_DECAY_LONG, math.log(HY_TARGET) / HY_DECAY_SHORT,
                                  HY_WIDTH, dtype=F32))
    return hid, t_unit, w3, deltas


def _hyena_filter(n, lp):
    na = n // DFT_B
    hid, t_unit, w3, deltas = _filter_features(n, lp)
    lag_f = DFT_B * np.arange(na)[None, :] + np.arange(DFT_B)[:, None]
    lag_b = n - lag_f
    zero_tap = jnp.asarray(lag_b == n)[:, :, None]
    lag_b = np.minimum(lag_b, n - 1)

    def taps(lags, w_dir):
        flat = lags.reshape(-1)
        vals = (hid[flat] @ w_dir) * jnp.exp(-t_unit[flat][:, None] * deltas)
        return vals.reshape(DFT_B, na, HY_WIDTH)

    orders = []
    for o in range(HY_ORDER):
        fwd = taps(lag_f, w3[:, 0, o])
        fwd = fwd.at[0, 0].add(hid[0] @ w3[:, 1, o])
        bwd = jnp.where(zero_tap, 0.0, taps(lag_b, w3[:, 1, o]))
        orders.append(jnp.concatenate([fwd, bwd], axis=1))
    g = jnp.stack(orders)
    return g * lax.rsqrt(jnp.sum(g * g, axis=(1, 2), keepdims=True) + EPS)


def _stack3(m):
    m32 = jnp.asarray(np.ascontiguousarray(m), F32)
    hi = m32.astype(BF16)
    lo = (m32 - hi.astype(F32)).astype(BF16)
    return jnp.concatenate([hi, lo, hi], axis=1)


def _dft_tables(n):
    na = n // DFT_B
    dft_a = 2 * na
    ka = np.arange(dft_a)[:, None]
    a = np.arange(dft_a)[None, :]
    th = 2.0 * np.pi * (ka * a % dft_a) / dft_a
    ca, sa = np.cos(th), np.sin(th)
    kb = np.arange(DFT_B)[:, None]
    tb = 2.0 * np.pi * (kb * kb.T % DFT_B) / DFT_B
    cb, sb = np.cos(tb), np.sin(tb)
    fa_full = np.concatenate([ca, -sa], axis=0)
    bb = np.arange(DFT_B)[:, None]
    kk = np.arange(dft_a)[None, :]
    ph = 2.0 * np.pi * (bb * kk) / (2 * n)
    tw = np.stack([np.cos(ph), -np.sin(ph)], axis=1)
    tw = np.broadcast_to(tw[..., None], (DFT_B, 2, dft_a, LANES))
    can, san = ca[:, :na], sa[:, :na]
    return dict(
        fa=_stack3(np.block([[can, san], [-san, can]])),
        fa_full=_stack3(fa_full),
        mf=_stack3(np.block([[cb, sb], [-sb, cb]])),
        mi=_stack3(np.block([[cb, -sb], [sb, cb]])),
        fc=_stack3(np.block([[can.T, -san.T], [san.T, can.T]]) / (2 * n)),
        tw=jnp.asarray(np.ascontiguousarray(tw), F32))


def _ctx_dft_tables(n):
    k = np.arange(2 * n)[:, None]
    t = np.arange(2 * n)[None, :]
    th = 2.0 * np.pi * (k * t % (2 * n)) / (2 * n)
    fw_full = np.concatenate([np.cos(th), -np.sin(th)], axis=0)
    iv = np.concatenate([np.cos(th)[:n], -np.sin(th)[:n]], axis=1) / (2 * n)
    return dict(fw=_stack3(fw_full[:, :n]), fw_full=_stack3(fw_full), iv=_stack3(iv))


def kernel(x, c, ctx, c_ctx, mod_w, mod_b, norm_mix_pre, norm_mix_post, norm_ffn_pre, norm_ffn_post, w_in,
           da_lambda, da_subln, ssd_conv_w, ssd_conv_b, ssd_a_log, ssd_dt_bias, ssd_d, ssd_norm, mla_q_norm,
           mla_w_uq, mla_kv_norm, mla_w_ukv, hy_conv_w, hy_conv_b, hy_w1, hy_b1, hy_freq1, hy_w2, hy_b2, hy_freq2,
           hy_w3, hy_bias, w_br_da, w_br_ssd, w_br_mla, w_br_hy, w_out, ffn_w1, ffn_w3, ffn_w2):
    bsz, n_lat, _ = x.shape
    n_ctx = ctx.shape[1]
    depth = w_in.shape[0]
    rope_tabs = _rope_tables(n_lat)
    dft_tabs = _dft_tables(n_lat)
    ctx_tabs = _ctx_dft_tables(n_ctx)
    row = lambda v: v.reshape(1, -1).astype(F32)

    c_rows = jnp.concatenate([c, c_ctx[None, :], jnp.zeros((SUBLANES - bsz - 1, D_MODEL), F32)], axis=0)
    o_da, o_ssd, o_mla, o_hy = np.cumsum([COLS_DA, COLS_SSD, COLS_MLA, COLS_HY]).tolist()

    x_lat, x_ctx = x, ctx
    for l in range(depth):
        last = l == depth - 1
        lam_init = 0.8 - 0.6 * math.exp(-0.3 * l)
        mods = _modulation(c_rows, mod_w[l].astype(BF16), row(mod_b[l]))
        m_lat = [mods[:bsz, k * D_MODEL:(k + 1) * D_MODEL][:, None, :] for k in range(6)]
        m_ctx = [jnp.broadcast_to(mods[bsz, k * D_MODEL:(k + 1) * D_MODEL][None, None, :], (bsz, 1, D_MODEL))
                 for k in range(6)]

        wl = w_in[l]
        w_da_qk = wl[:, :2 * DA_QK_COLS].astype(BF16)
        w_da_vt = wl[:, 2 * DA_QK_COLS:o_da].T.astype(BF16)
        w_ssd = jnp.pad(wl[:, o_da:o_ssd], ((0, 0), (0, 1664 - COLS_SSD))).astype(BF16)
        w_mla = jnp.pad(wl[:, o_ssd:o_mla], ((0, 0), (0, 768 - COLS_MLA))).astype(BF16)
        w_hy = wl[:, o_mla:o_hy].astype(BF16)
        w_gate = wl[:, o_hy:].astype(BF16)
        wuq = mla_w_uq[l].reshape(MLA_Q_LORA, MLA_HEADS, MLA_NOPE + MLA_ROPE)
        wuq = jnp.pad(wuq, ((0, 0), (0, 0), (0, 256 - MLA_NOPE - MLA_ROPE))).reshape(MLA_Q_LORA, 1024).astype(BF16)
        wukv = mla_w_ukv[l].reshape(MLA_KV_LORA, MLA_HEADS, MLA_NOPE + MLA_V)
        wuk = wukv[:, :, :MLA_NOPE].reshape(MLA_KV_LORA, MLA_HEADS * MLA_NOPE).astype(BF16)
        wuvt = wukv[:, :, MLA_NOPE:].reshape(MLA_KV_LORA, MLA_WIDTH).T.astype(BF16)
        g_pre = row(norm_mix_pre[l])

        lv = da_lambda[l].astype(F32)
        lam = (jnp.exp(jnp.sum(lv[0] * lv[1])) - jnp.exp(jnp.sum(lv[2] * lv[3])) + lam_init).reshape(1, 1)
        da_fin = (lam, da_subln[l].reshape(-1, 1).astype(F32), lam_init)

        dtb = jnp.pad(ssd_dt_bias[l].reshape(1, -1), ((0, 0), (0, LANES - 2 * SSD_HEADS))).astype(F32)
        a_row = jnp.pad(-jnp.exp(ssd_a_log[l].astype(F32)).reshape(1, -1), ((0, 0), (0, LANES - 2 * SSD_HEADS)))
        d_row = row(jnp.repeat(ssd_d[l], SSD_HEAD_DIM))
        g_ssd = row(ssd_norm[l])

        lp = {'hy_w1': hy_w1[l], 'hy_b1': hy_b1[l], 'hy_freq1': hy_freq1[l], 'hy_w2': hy_w2[l],
              'hy_b2': hy_b2[l], 'hy_freq2': hy_freq2[l], 'hy_w3': hy_w3[l]}
        hy_bias_l = hy_bias[l].astype(F32)

        need_ctx = not last
        w_proj = (w_da_qk, w_da_vt, w_ssd, w_mla, row(mla_q_norm[l]), wuq, row(mla_kv_norm[l]), wuk, wuvt, w_hy)
        pl_ = _inproj(x_lat, g_pre, m_lat[0], m_lat[1], w_proj, rope_tabs)
        pc_ = _inproj(x_ctx, g_pre, m_ctx[0], m_ctx[1], w_proj, None)

        da_lat = _flash(pl_['q_da'], pc_['k_da'], pc_['v_da'], pl_['k_da'], pl_['v_da'],
                        ncomp=2, heads=DA_HEADS, dq=LANES, da=da_fin)
        mla_lat = _flash(pl_['q_m'], pc_['k_m'], pc_['v_m'], pl_['k_m'], pl_['v_m'],
                         ncomp=1, heads=MLA_HEADS, dq=256)
        cw, cb_ = ssd_conv_w[l].astype(F32), row(ssd_conv_b[l])
        xc_c = _dwconv(pc_['xbc'], cw, cb_, 1, True)
        xc_l = _dwconv(pl_['xbc'], cw, cb_, 1, True)
        zeros = jnp.zeros((bsz, 4, 128, 128), F32)
        yf_c, sf = _ssd_scan(xc_c, pc_['dt'], zeros, dtb, a_row, False)
        yf_l, _ = _ssd_scan(xc_l, pl_['dt'], sf, dtb, a_row, False)
        ssd_ctx, sb = _ssd_scan(xc_c, pc_['dt'], zeros, dtb, a_row, True, fin=(yf_c, pc_['z'], d_row, g_ssd))
        ssd_lat, _ = _ssd_scan(xc_l, pl_['dt'], sb, dtb, a_row, True, fin=(yf_l, pl_['z'], d_row, g_ssd))
        hw, hb = hy_conv_w[l].astype(F32), row(hy_conv_b[l])
        u_l = _dwconv(pl_['hy'], hw, hb, 3, False)
        hid, t_unit, w3f, deltas = _filter_features(n_lat, lp)
        g_l, g_ss = _hy_filter(hid, t_unit, w3f.transpose(1, 2, 0, 3).astype(BF16), deltas.reshape(1, -1), n_lat)
        spec_l = _hy_spectrum(g_l, g_ss, dft_tabs)
        u5 = u_l.reshape(3, bsz, n_lat // DFT_B, DFT_B, HY_WIDTH)
        z1 = _hy_long_conv(u5, 0, u5, 1, spec_l[0], hy_bias_l[0:1], dft_tabs)
        hy_lat = _hy_long_conv(z1[None], 0, u5, 2, spec_l[1], hy_bias_l[1:2], dft_tabs).reshape(bsz, n_lat, HY_WIDTH)

        wbr = jnp.stack([w_br_da[l], w_br_ssd[l], w_br_mla[l], w_br_hy[l]]).astype(BF16)
        wout = w_out[l].astype(BF16)
        g_post = row(norm_mix_post[l])
        w1, w3, w2 = ffn_w1[l].astype(BF16), ffn_w3[l].astype(BF16), ffn_w2[l].astype(BF16)
        gf_pre, gf_post = row(norm_ffn_pre[l]), row(norm_ffn_post[l])

        x_lat = _merge(x_lat, g_pre, m_lat[0], m_lat[1], (da_lat, ssd_lat, mla_lat, hy_lat), w_gate, wbr, wout,
                       g_post, m_lat[2])
        x_lat = _ffn(x_lat, gf_pre, m_lat[3], m_lat[4], w1, w3, w2, gf_post, m_lat[5])

        if need_ctx:
            da_ctx = _flash(pc_['q_da'], pc_['k_da'], pc_['v_da'], None, None,
                            ncomp=2, heads=DA_HEADS, dq=LANES, da=da_fin)
            mla_ctx = _flash(pc_['q_m'], pc_['k_m'], pc_['v_m'], None, None, ncomp=1, heads=MLA_HEADS, dq=256)
            u_c = _dwconv(pc_['hy'], hw, hb, 3, False)
            g_c = _hyena_filter(n_ctx, lp).transpose(2, 1, 0, 3).reshape(2 * n_ctx, HY_ORDER * HY_WIDTH)
            spec_c = _ctx_spectrum(g_c, ctx_tabs['fw_full'])
            hy_ctx = _hy_ctx(u_c, spec_c, ctx_tabs['fw'], ctx_tabs['iv'], hy_bias_l)
            x_ctx = _merge(x_ctx, g_pre, m_ctx[0], m_ctx[1], (da_ctx, ssd_ctx, mla_ctx, hy_ctx), w_gate, wbr, wout,
                           g_post, m_ctx[2])
            x_ctx = _ffn(x_ctx, gf_pre, m_ctx[3], m_ctx[4], w1, w3, w2, gf_post, m_ctx[5])
    return x_lat
```
